```python
import jax, jax.numpy as jnp
from jax import lax
import numpy as np

D_MODEL = 1024
BATCH = 2
SEQ = 8192
DEPTH = 2

N_META = 16
DN_HEADS = 8
DN_HEAD_DIM = 128
DN_WIDTH = DN_HEADS * DN_HEAD_DIM
DN_CONV = 4
CHUNK = 64
SC_WIDTH = D_MODEL
SC_CONV = 3
D_FF = -(-8 * D_MODEL // (3 * 256)) * 256
EPS = 1e-6
PROJ_WIDTHS = (DN_WIDTH, DN_WIDTH, DN_WIDTH, DN_WIDTH, DN_HEADS, DN_HEADS,
               SC_WIDTH, SC_WIDTH, SC_WIDTH, D_MODEL, D_MODEL)
IN_WIDTH = sum(PROJ_WIDTHS)

kernel_name = "hybrid_gdn_shortconv_meta_block"


def rms_norm(x, w):
    xf = x.astype(jnp.float32)
    y = xf * lax.rsqrt(jnp.mean(xf * xf, axis=-1, keepdims=True) + EPS)
    return (y * w.astype(jnp.float32)).astype(x.dtype)


def l2_normalize(x):
    xf = x.astype(jnp.float32)
    return xf * lax.rsqrt(jnp.sum(xf * xf, axis=-1, keepdims=True) + EPS)


def split_columns(t, widths):
    offs = np.cumsum(np.array(widths))[:-1].tolist()
    return jnp.split(t, offs, axis=-1)


def causal_depthwise_conv(x, w):
    k_len, ch = w.shape
    return lax.conv_general_dilated(
        x, w[:, None, :].astype(x.dtype), window_strides=(1,), padding=[(k_len - 1, 0)],
        dimension_numbers=('NWC', 'WIO', 'NWC'), feature_group_count=ch)


def chunk_gated_delta_rule(q, k, v, g, beta):
    bsz, seq_len, n_heads, dk = q.shape
    dv = v.shape[-1]
    pad = (-seq_len) % CHUNK
    n_chunks = (seq_len + pad) // CHUNK

    def front_pad(t):
        return jnp.pad(t, [(0, 0), (pad, 0)] + [(0, 0)] * (t.ndim - 2))

    def to_chunks(t):
        t = jnp.swapaxes(front_pad(t), 1, 2)
        return t.reshape(t.shape[:2] + (n_chunks, CHUNK) + t.shape[3:])

    q = to_chunks(q) * (dk ** -0.5)
    k = to_chunks(k)
    v = to_chunks(v)
    g = to_chunks(g)
    beta = to_chunks(beta)

    gc = jnp.cumsum(g, axis=-1)
    causal = jnp.tril(jnp.ones((CHUNK, CHUNK), dtype=bool))
    strict = jnp.tril(jnp.ones((CHUNK, CHUNK), dtype=bool), -1)
    decay = jnp.exp(jnp.where(causal, gc[..., :, None] - gc[..., None, :], -jnp.inf))

    k_beta = k * beta[..., None]
    v_beta = v * beta[..., None]
    a_mat = jnp.where(strict, jnp.einsum('bhncd,bhnsd->bhncs', k_beta, k) * decay, 0.0)
    eye = jnp.eye(CHUNK, dtype=a_mat.dtype)
    rhs = jnp.concatenate([v_beta, k_beta * jnp.exp(gc)[..., None]], axis=-1)
    sol = lax.linalg.triangular_solve(a_mat + eye, rhs, left_side=True, lower=True)
    u_val, w_cum = sol[..., :dv], sol[..., dv:]

    qk = jnp.einsum('bhncd,bhnsd->bhncs', q, k) * decay
    q_dec = q * jnp.exp(gc)[..., None]
    k_dec = k * jnp.exp(gc[..., -1:] - gc)[..., None]
    g_last = jnp.exp(gc[..., -1])

    def step(state, xs):
        q_i, k_i, u_i, w_i, qk_i, gl_i = xs
        v_new = u_i - jnp.einsum('bhcd,bhde->bhce', w_i, state)
        o_i = jnp.einsum('bhcd,bhde->bhce', q_i, state) + jnp.einsum('bhcs,bhse->bhce', qk_i, v_new)
        state = state * gl_i[..., None, None] + jnp.einsum('bhcd,bhce->bhde', k_i, v_new)
        return state, o_i

    xs = tuple(jnp.moveaxis(t, 2, 0) for t in (q_dec, k_dec, u_val, w_cum, qk, g_last))
    state0 = jnp.zeros((bsz, n_heads, dk, dv), dtype=jnp.float32)
    _, o = lax.scan(step, state0, xs)
    o = jnp.transpose(o, (1, 0, 3, 2, 4)).reshape(bsz, n_chunks * CHUNK, n_heads, dv)
    return o[:, pad:]


def hybrid_layer(h, norm1, w_in, conv_qkv, a_log, dt_bias, o_norm, w_dn_out, conv_sc,
                 w_sc_out, w_o, norm2, w_gate_up, w_down):
    bsz, seq_len, _ = h.shape
    xn = rms_norm(h, norm1)
    proj = xn @ w_in
    q, k, v, z, b_raw, a_raw, sc_c, sc_u, sc_b, gate_a, gate_b = split_columns(proj, PROJ_WIDTHS)

    qkv = jax.nn.silu(causal_depthwise_conv(jnp.concatenate([q, k, v], axis=-1), conv_qkv))
    q, k, v = jnp.split(qkv, 3, axis=-1)
    hd = (bsz, seq_len, DN_HEADS, DN_HEAD_DIM)
    q = l2_normalize(q.reshape(hd))
    k = l2_normalize(k.reshape(hd))
    v = v.reshape(hd).astype(jnp.float32)
    beta = jax.nn.sigmoid(b_raw.astype(jnp.float32))
    g = -jnp.exp(a_log.astype(jnp.float32)) * jax.nn.softplus(
        a_raw.astype(jnp.float32) + dt_bias.astype(jnp.float32))
    o = chunk_gated_delta_rule(q, k, v, g, beta)
    o = rms_norm(o, o_norm) * jax.nn.silu(z.reshape(hd).astype(jnp.float32))
    y_a = o.reshape(bsz, seq_len, DN_WIDTH).astype(h.dtype) @ w_dn_out

    y_b = (sc_b * causal_depthwise_conv(sc_c * sc_u, conv_sc)) @ w_sc_out

    mixed = jax.nn.sigmoid(gate_a) * y_a + jax.nn.sigmoid(gate_b) * y_b
    h = h + mixed @ w_o

    gt, up = jnp.split(rms_norm(h, norm2) @ w_gate_up, 2, axis=-1)
    return h + (jax.nn.silu(gt) * up) @ w_down


def setup_inputs(seed: int = 0) -> dict:
    key = jax.random.key(seed)
    ks = jax.random.split(key, 20)
    f32 = jnp.float32

    def nrm(k, shape, scale):
        return jax.random.normal(k, shape, f32) * scale

    def gain(k, shape):
        return 1.0 + 0.02 * jax.random.normal(k, shape, f32)

    dt = jnp.exp(jax.random.uniform(ks[5], (DEPTH, DN_HEADS), f32, np.log(1e-3), np.log(1e-1)))
    return {
        "x": nrm(ks[0], (BATCH, SEQ, D_MODEL), 1.0),
        "meta_tokens": nrm(ks[1], (N_META, D_MODEL), 1.0),
        "norm1": gain(ks[2], (DEPTH, D_MODEL)),
        "w_in": nrm(ks[3], (DEPTH, D_MODEL, IN_WIDTH), D_MODEL ** -0.5),
        "conv_qkv": nrm(ks[4], (DEPTH, DN_CONV, 3 * DN_WIDTH), DN_CONV ** -0.5),
        "a_log": jnp.log(jax.random.uniform(ks[6], (DEPTH, DN_HEADS), f32, 1.0, 16.0)),
        "dt_bias": dt + jnp.log(-jnp.expm1(-dt)),
        "o_norm": gain(ks[7], (DEPTH, DN_HEAD_DIM)),
        "w_dn_out": nrm(ks[8], (DEPTH, DN_WIDTH, D_MODEL), DN_WIDTH ** -0.5),
        "conv_sc": nrm(ks[9], (DEPTH, SC_CONV, SC_WIDTH), SC_CONV ** -0.5),
        "w_sc_out": nrm(ks[10], (DEPTH, SC_WIDTH, D_MODEL), SC_WIDTH ** -0.5),
        "w_o": nrm(ks[11], (DEPTH, D_MODEL, D_MODEL), D_MODEL ** -0.5),
        "norm2": gain(ks[12], (DEPTH, D_MODEL)),
        "w_gate_up": nrm(ks[13], (DEPTH, D_MODEL, 2 * D_FF), D_MODEL ** -0.5),
        "w_down": nrm(ks[14], (DEPTH, D_FF, D_MODEL), D_FF ** -0.5),
        "final_norm": gain(ks[15], (D_MODEL,)),
    }


def reference(x, meta_tokens, norm1, w_in, conv_qkv, a_log, dt_bias, o_norm, w_dn_out, conv_sc,
              w_sc_out, w_o, norm2, w_gate_up, w_down, final_norm):
    bsz = x.shape[0]
    meta = jnp.broadcast_to(meta_tokens[None].astype(x.dtype), (bsz, N_META, D_MODEL))
    h = jnp.concatenate([meta, x], axis=1)
    for layer in range(DEPTH):
        h = hybrid_layer(h, norm1[layer], w_in[layer], conv_qkv[layer], a_log[layer],
                         dt_bias[layer], o_norm[layer], w_dn_out[layer], conv_sc[layer],
                         w_sc_out[layer], w_o[layer], norm2[layer], w_gate_up[layer],
                         w_down[layer])
    return rms_norm(h, final_norm)[:, N_META:]
```

```python
import functools

import jax
import jax.numpy as jnp
from jax import lax
from jax.experimental import pallas as pl
from jax.experimental.pallas import tpu as pltpu

D_MODEL = 1024
N_META = 16
DN_HEADS = 8
DN_HEAD_DIM = 128
DN_CONV = 4
SC_CONV = 3
CHUNK = 64
D_FF = 2816
EPS = 1e-6

LANES = 128
SUBLANES = 8
TM = 256
TC = 256
VMEM_LIMIT = 56 * 1024 * 1024

OFF_Q, OFF_K, OFF_V, OFF_Z = 0, 1024, 2048, 3072
OFF_BA = 4096
OFF_C, OFF_U, OFF_B = 4224, 5248, 6272
OFF_GA, OFF_GB = 7296, 8320
PACKED_WIDTH = 9344

F32 = jnp.float32
BF16 = jnp.bfloat16


def _bdot(a, b):
    return jnp.dot(a.astype(BF16), b.astype(BF16), preferred_element_type=F32)


def _silu(x):
    return x * jax.nn.sigmoid(x)


def _inproj_kernel(h_ref, n1_ref, w_ref, cq_ref, csc_ref, alog_ref, dtb_ref,
                   q_ref, k_ref, v_ref, zs_ref, gates_ref, gatest_ref, s_ref, ga_ref, gb_ref,
                   stage_ref, halo_qkv_ref, halo_sc_ref, *, tiles_per_batch):
    i = pl.program_id(0)
    tm = h_ref.shape[0]

    @pl.when(i % tiles_per_batch == 0)
    def _():
        halo_qkv_ref[...] = jnp.zeros_like(halo_qkv_ref)
        halo_sc_ref[...] = jnp.zeros_like(halo_sc_ref)

    x = h_ref[...]
    xn = x * lax.rsqrt(jnp.mean(x * x, axis=-1, keepdims=True) + EPS) * n1_ref[...]
    xn = xn.astype(BF16)

    def proj(off, width):
        return jnp.dot(xn, w_ref[:, off:off + width], preferred_element_type=F32)

    def causal_conv(raw, halo_ref, col, taps_ref, n_taps):
        width = raw.shape[1]
        stage_ref[0:SUBLANES, 0:width] = halo_ref[:, col:col + width]
        stage_ref[SUBLANES:SUBLANES + tm, 0:width] = raw
        halo_ref[:, col:col + width] = raw[tm - SUBLANES:tm, :]
        acc = raw * taps_ref[n_taps - 1:n_taps, col:col + width]
        for j in range(n_taps - 1):
            start = SUBLANES - (n_taps - 1) + j
            acc = acc + stage_ref[start:start + tm, 0:width] * taps_ref[j:j + 1, col:col + width]
        return acc

    def head_l2norm(y, scale):
        outs = []
        for hd in range(DN_HEADS):
            yh = y[:, hd * DN_HEAD_DIM:(hd + 1) * DN_HEAD_DIM]
            inv = lax.rsqrt(jnp.sum(yh * yh, axis=-1, keepdims=True) + EPS)
            outs.append(yh * (inv * scale))
        return jnp.concatenate(outs, axis=-1)

    yq = _silu(causal_conv(proj(OFF_Q, 1024), halo_qkv_ref, 0, cq_ref, DN_CONV))
    q_ref[...] = head_l2norm(yq, DN_HEAD_DIM ** -0.5).astype(BF16)
    yk = _silu(causal_conv(proj(OFF_K, 1024), halo_qkv_ref, 1024, cq_ref, DN_CONV))
    k_ref[...] = head_l2norm(yk, 1.0).astype(BF16)
    yv = _silu(causal_conv(proj(OFF_V, 1024), halo_qkv_ref, 2048, cq_ref, DN_CONV))
    v_ref[...] = yv.astype(BF16)

    zs_ref[...] = _silu(proj(OFF_Z, 1024)).astype(BF16)

    ba = proj(OFF_BA, LANES)
    lane = lax.broadcasted_iota(jnp.int32, ba.shape, 1)
    g = -jnp.exp(alog_ref[...]) * jax.nn.softplus(ba + dtb_ref[...])
    gates = jnp.where(lane < DN_HEADS, jax.nn.sigmoid(ba), jnp.where(lane < 2 * DN_HEADS, g, 0.0))
    gates_ref[...] = gates
    gatest_ref[...] = gates.T[0:2 * DN_HEADS, :]

    cu = proj(OFF_C, 1024) * proj(OFF_U, 1024)
    s_ref[...] = (proj(OFF_B, 1024) * causal_conv(cu, halo_sc_ref, 0, csc_ref, SC_CONV)).astype(BF16)

    ga_ref[...] = jax.nn.sigmoid(proj(OFF_GA, 1024)).astype(BF16)
    gb_ref[...] = jax.nn.sigmoid(proj(OFF_GB, 1024)).astype(BF16)


def _resident(shape):
    return pl.BlockSpec(shape, lambda *_: (0,) * len(shape), pipeline_mode=pl.Buffered(1))


def _inproj(h, n1, w_packed, cq, csc, alog_row, dtb_row, tiles_per_batch):
    rows = h.shape[0]
    n_tiles = rows // TM
    row_spec = pl.BlockSpec((TM, D_MODEL), lambda i: (i, 0))
    act = jax.ShapeDtypeStruct((rows, D_MODEL), BF16)
    return pl.pallas_call(
        functools.partial(_inproj_kernel, tiles_per_batch=tiles_per_batch),
        grid=(n_tiles,),
        in_specs=[
            row_spec,
            _resident((1, D_MODEL)),
            _resident((D_MODEL, PACKED_WIDTH)),
            _resident((DN_CONV, 3 * D_MODEL)),
            _resident((SC_CONV, D_MODEL)),
            _resident((1, LANES)),
            _resident((1, LANES)),
        ],
        out_specs=[
            row_spec, row_spec, row_spec, row_spec,
            pl.BlockSpec((TM, LANES), lambda i: (i, 0)),
            pl.BlockSpec((2 * DN_HEADS, TM), lambda i: (0, i)),
            row_spec, row_spec, row_spec,
        ],
        out_shape=[
            act, act, act, act,
            jax.ShapeDtypeStruct((rows, LANES), F32),
            jax.ShapeDtypeStruct((2 * DN_HEADS, rows), F32),
            act, act, act,
        ],
        scratch_shapes=[
            pltpu.VMEM((SUBLANES + TM, D_MODEL), F32),
            pltpu.VMEM((SUBLANES, 3 * D_MODEL), F32),
            pltpu.VMEM((SUBLANES, D_MODEL), F32),
        ],
        compiler_params=pltpu.CompilerParams(
            dimension_semantics=("arbitrary",), vmem_limit_bytes=VMEM_LIMIT),
        name="inproj",
    )(h, n1, w_packed, cq, csc, alog_row, dtb_row)


def _split3(x):
    x1 = x.astype(BF16)
    r1 = x - x1.astype(F32)
    x2 = r1.astype(BF16)
    x3 = (r1 - x2.astype(F32)).astype(BF16)
    return x1, x2, x3


def _delta_kernel(q_ref, k_ref, v_ref, zs_ref, gates_ref, gatest_ref, onorm_ref, og_ref, state_ref):
    @pl.when(pl.program_id(1) == 0)
    def _():
        state_ref[...] = jnp.zeros_like(state_ref)

    n_chunks = q_ref.shape[0] // CHUNK
    row = lax.broadcasted_iota(jnp.int32, (CHUNK, CHUNK), 0)
    col = lax.broadcasted_iota(jnp.int32, (CHUNK, CHUNK), 1)
    causal = row >= col
    strict = row > col
    same16 = (row // 16) == (col // 16)
    same32 = (row // 32) == (col // 32)
    tril = jnp.where(causal, 1.0, 0.0).astype(BF16)
    triu = jnp.where(row <= col, 1.0, 0.0).astype(BF16)
    eye = jnp.where(row == col, 1.0, 0.0)
    onorm = onorm_ref[...]

    def chunk_body(c, carry):
        r0 = pl.multiple_of(c * CHUNK, CHUNK)
        gates = gates_ref[pl.ds(r0, CHUNK), :]
        g1, g2, g3 = _split3(gates)
        gcol = (jnp.dot(tril, g1, preferred_element_type=F32)
                + jnp.dot(tril, g2, preferred_element_type=F32)
                + jnp.dot(tril, g3, preferred_element_type=F32))
        t1, t2, t3 = _split3(gatest_ref[c])
        grow_all = (jnp.dot(t1, triu, preferred_element_type=F32)
                    + jnp.dot(t2, triu, preferred_element_type=F32)
                    + jnp.dot(t3, triu, preferred_element_type=F32))

        for hd in range(DN_HEADS):
            cols = slice(hd * DN_HEAD_DIM, (hd + 1) * DN_HEAD_DIM)
            q = q_ref[pl.ds(r0, CHUNK), cols].astype(F32)
            k = k_ref[pl.ds(r0, CHUNK), cols].astype(F32)
            v = v_ref[pl.ds(r0, CHUNK), cols].astype(F32)
            beta = jnp.broadcast_to(gates[:, hd:hd + 1], (CHUNK, DN_HEAD_DIM))
            gc = jnp.broadcast_to(gcol[:, DN_HEADS + hd:DN_HEADS + hd + 1], (CHUNK, DN_HEAD_DIM))
            grow = grow_all[DN_HEADS + hd:DN_HEADS + hd + 1, :]
            g_last = gc[CHUNK - 1:CHUNK, :]

            decay = jnp.exp(jnp.where(causal, gc[:, 0:CHUNK] - grow, -jnp.inf))
            e_gc = jnp.exp(gc)
            kb = k * beta
            vb = v * beta
            kbd = kb * e_gc
            qd = q * e_gc
            kdec = k * jnp.exp(g_last - gc)

            kq = lax.dot_general(jnp.concatenate([kb, q], axis=0).astype(BF16), k.astype(BF16),
                                 (((1,), (1,)), ((), ())), preferred_element_type=F32)
            a_mat = jnp.where(strict, kq[0:CHUNK] * decay, 0.0)
            qk = kq[CHUNK:2 * CHUNK] * decay

            n1 = jnp.where(same16, -a_mat, 0.0)
            n2 = _bdot(n1, n1)
            n4 = _bdot(n2, n2)
            n8 = _bdot(n4, n4)
            t = eye + n1
            t = t + _bdot(t, n2)
            t = t + _bdot(t, n4)
            t = t + _bdot(t, n8)
            a1 = jnp.where(same32 & jnp.logical_not(same16), a_mat, 0.0)
            t = t - _bdot(t, _bdot(a1, t))
            a2 = jnp.where(same32, 0.0, a_mat)
            t = t - _bdot(t, _bdot(a2, t))

            uw = _bdot(t, jnp.concatenate([vb, kbd], axis=1))
            u = uw[:, 0:DN_HEAD_DIM]
            w = uw[:, DN_HEAD_DIM:2 * DN_HEAD_DIM]

            state = state_ref[hd]
            ws_qs = _bdot(jnp.concatenate([w, qd], axis=0), state)
            v_new = u - ws_qs[0:CHUNK]
            o = ws_qs[CHUNK:2 * CHUNK] + _bdot(qk, v_new)
            state_ref[hd] = state * jnp.exp(g_last) + lax.dot_general(
                kdec.astype(BF16), v_new.astype(BF16), (((0,), (0,)), ((), ())),
                preferred_element_type=F32)

            o = o * lax.rsqrt(jnp.mean(o * o, axis=-1, keepdims=True) + EPS) * onorm
            og_ref[pl.ds(r0, CHUNK), cols] = (o * zs_ref[pl.ds(r0, CHUNK), cols].astype(F32)).astype(BF16)
        return carry

    lax.fori_loop(0, n_chunks, chunk_body, 0)


def _delta(q, k, v, zs, gates, gatest_chunks, onorm, bsz):
    rows = q.shape[0]
    steps = rows // bsz // TC
    row_spec = pl.BlockSpec((TC, D_MODEL), lambda b, i: (b * steps + i, 0))
    return pl.pallas_call(
        _delta_kernel,
        grid=(bsz, steps),
        in_specs=[
            row_spec, row_spec, row_spec, row_spec,
            pl.BlockSpec((TC, LANES), lambda b, i: (b * steps + i, 0)),
            pl.BlockSpec((TC // CHUNK, 2 * DN_HEADS, CHUNK), lambda b, i: (b * steps + i, 0, 0)),
            pl.BlockSpec((1, DN_HEAD_DIM), lambda b, i: (0, 0)),
        ],
        out_specs=row_spec,
        out_shape=jax.ShapeDtypeStruct((rows, D_MODEL), BF16),
        scratch_shapes=[pltpu.VMEM((DN_HEADS, DN_HEAD_DIM, DN_HEAD_DIM), F32)],
        compiler_params=pltpu.CompilerParams(
            dimension_semantics=("arbitrary", "arbitrary"), vmem_limit_bytes=VMEM_LIMIT),
        name="delta_rule",
    )(q, k, v, zs, gates, gatest_chunks, onorm)


FF_SPLITS = ((0, 1024), (1024, 2048), (2048, D_FF))


def _ffn_kernel(h_ref, og_ref, s_ref, ga_ref, gb_ref, wdn_ref, wsc_ref, wo_ref, n2_ref,
                wgu_ref, wdown_ref, fnorm_ref, out_ref, *, final):
    ya = jnp.dot(og_ref[...], wdn_ref[...], preferred_element_type=F32)
    yb = jnp.dot(s_ref[...], wsc_ref[...], preferred_element_type=F32)
    mixed = ga_ref[...].astype(F32) * ya + gb_ref[...].astype(F32) * yb
    h1 = h_ref[...] + _bdot(mixed, wo_ref[...])
    hn = h1 * lax.rsqrt(jnp.mean(h1 * h1, axis=-1, keepdims=True) + EPS) * n2_ref[...]
    hn = hn.astype(BF16)
    acc = h1
    for lo, hi in FF_SPLITS:
        gate = jnp.dot(hn, wgu_ref[:, lo:hi], preferred_element_type=F32)
        up = jnp.dot(hn, wgu_ref[:, D_FF + lo:D_FF + hi], preferred_element_type=F32)
        acc = acc + _bdot(_silu(gate) * up, wdown_ref[lo:hi, :])
    if final:
        acc = acc * lax.rsqrt(jnp.mean(acc * acc, axis=-1, keepdims=True) + EPS) * fnorm_ref[...]
    out_ref[...] = acc


def _ffn(h, og, s, ga, gb, wdn, wsc, wo, n2, wgu, wdown, fnorm, final):
    rows = h.shape[0]
    row_spec = pl.BlockSpec((TM, D_MODEL), lambda i: (i, 0))
    return pl.pallas_call(
        functools.partial(_ffn_kernel, final=final),
        grid=(rows // TM,),
        in_specs=[
            row_spec, row_spec, row_spec, row_spec, row_spec,
            _resident((D_MODEL, D_MODEL)), _resident((D_MODEL, D_MODEL)), _resident((D_MODEL, D_MODEL)),
            _resident((1, D_MODEL)),
            _resident((D_MODEL, 2 * D_FF)), _resident((D_FF, D_MODEL)),
            _resident((1, D_MODEL)),
        ],
        out_specs=row_spec,
        out_shape=jax.ShapeDtypeStruct((rows, D_MODEL), F32),
        compiler_params=pltpu.CompilerParams(
            dimension_semantics=("arbitrary",), vmem_limit_bytes=VMEM_LIMIT),
        name="mix_ffn",
    )(h, og, s, ga, gb, wdn, wsc, wo, n2, wgu, wdown, fnorm)


def _pack_w_in(w):
    ba = jnp.pad(w[:, 4096:4112], ((0, 0), (0, LANES - 2 * DN_HEADS)))
    return jnp.concatenate([w[:, 0:4096], ba, w[:, 4112:]], axis=1).astype(BF16)


def _gate_row(p):
    return jnp.pad(p.astype(F32), (DN_HEADS, LANES - 2 * DN_HEADS)).reshape(1, LANES)


def kernel(x, meta_tokens, norm1, w_in, conv_qkv, a_log, dt_bias, o_norm, w_dn_out, conv_sc,
           w_sc_out, w_o, norm2, w_gate_up, w_down, final_norm):
    bsz, seq, d = x.shape
    depth = w_in.shape[0]
    assert d == D_MODEL and seq % TM == 0 and TM % TC == 0 and TM >= N_META
    lp = TM + seq
    tiles_per_batch = lp // TM

    front = jnp.zeros((bsz, TM - N_META, d), x.dtype)
    meta = jnp.broadcast_to(meta_tokens[None].astype(x.dtype), (bsz, N_META, d))
    h = jnp.concatenate([front, meta, x], axis=1).reshape(bsz * lp, d)

    for layer in range(depth):
        q, k, v, zs, gates, gatest, s, ga, gb = _inproj(
            h, norm1[layer].reshape(1, d), _pack_w_in(w_in[layer]), conv_qkv[layer], conv_sc[layer],
            _gate_row(a_log[layer]), _gate_row(dt_bias[layer]), tiles_per_batch)
        gatest_chunks = gatest.reshape(2 * DN_HEADS, -1, CHUNK).transpose(1, 0, 2)
        og = _delta(q, k, v, zs, gates, gatest_chunks, o_norm[layer].reshape(1, DN_HEAD_DIM), bsz)
        h = _ffn(h, og, s, ga, gb, w_dn_out[layer].astype(BF16), w_sc_out[layer].astype(BF16),
                 w_o[layer].astype(BF16), norm2[layer].reshape(1, d), w_gate_up[layer].astype(BF16),
                 w_down[layer].astype(BF16), final_norm.reshape(1, d), layer == depth - 1)

    return h.reshape(bsz, lp, d)[:, TM:]
```

```python
import functools

import jax
import jax.numpy as jnp
from jax import lax
from jax.experimental import pallas as pl
from jax.experimental.pallas import tpu as pltpu

D_MODEL = 1024
N_META = 16
DN_HEADS = 8
DN_HEAD_DIM = 128
DN_CONV = 4
SC_CONV = 3
CHUNK = 64
D_FF = 2816
EPS = 1e-6

LANES = 128
SUBLANES = 8
TM = 256
TC = 256
VMEM_LIMIT = 56 * 1024 * 1024

OFF_Q, OFF_K, OFF_V, OFF_Z = 0, 1024, 2048, 3072
OFF_BA = 4096
OFF_C, OFF_U, OFF_B = 4224, 5248, 6272
OFF_GA, OFF_GB = 7296, 8320
PACKED_WIDTH = 9344

F32 = jnp.float32
BF16 = jnp.bfloat16


def _bdot(a, b):
    return jnp.dot(a.astype(BF16), b.astype(BF16), preferred_element_type=F32)


def _silu(x):
    return x * jax.nn.sigmoid(x)


def _inproj_kernel(h_ref, n1_ref, w_ref, cq_ref, csc_ref, alog_ref, dtb_ref,
                   q_ref, k_ref, v_ref, zs_ref, gates_ref, gatest_ref, s_ref, ga_ref, gb_ref,
                   stage_ref, halo_qkv_ref, halo_sc_ref, *, tiles_per_batch):
    i = pl.program_id(0)
    tm = h_ref.shape[0]

    @pl.when(i % tiles_per_batch == 0)
    def _():
        halo_qkv_ref[...] = jnp.zeros_like(halo_qkv_ref)
        halo_sc_ref[...] = jnp.zeros_like(halo_sc_ref)

    x = h_ref[...]
    xn = x * lax.rsqrt(jnp.mean(x * x, axis=-1, keepdims=True) + EPS) * n1_ref[...]
    xn = xn.astype(BF16)

    def proj(off, width):
        return jnp.dot(xn, w_ref[:, off:off + width], preferred_element_type=F32)

    def causal_conv(raw, halo_ref, col, taps_ref, n_taps):
        width = raw.shape[1]
        stage_ref[0:SUBLANES, 0:width] = halo_ref[:, col:col + width]
        stage_ref[SUBLANES:SUBLANES + tm, 0:width] = raw
        halo_ref[:, col:col + width] = raw[tm - SUBLANES:tm, :]
        acc = raw * taps_ref[n_taps - 1:n_taps, col:col + width]
        for j in range(n_taps - 1):
            start = SUBLANES - (n_taps - 1) + j
            acc = acc + stage_ref[start:start + tm, 0:width] * taps_ref[j:j + 1, col:col + width]
        return acc

    def head_l2norm(y, scale):
        outs = []
        for hd in range(DN_HEADS):
            yh = y[:, hd * DN_HEAD_DIM:(hd + 1) * DN_HEAD_DIM]
            inv = lax.rsqrt(jnp.sum(yh * yh, axis=-1, keepdims=True) + EPS)
            outs.append(yh * (inv * scale))
        return jnp.concatenate(outs, axis=-1)

    yq = _silu(causal_conv(proj(OFF_Q, 1024), halo_qkv_ref, 0, cq_ref, DN_CONV))
    q_ref[...] = head_l2norm(yq, DN_HEAD_DIM ** -0.5).astype(BF16)
    yk = _silu(causal_conv(proj(OFF_K, 1024), halo_qkv_ref, 1024, cq_ref, DN_CONV))
    k_ref[...] = head_l2norm(yk, 1.0).astype(BF16)
    yv = _silu(causal_conv(proj(OFF_V, 1024), halo_qkv_ref, 2048, cq_ref, DN_CONV))
    v_ref[...] = yv.astype(BF16)

    zs_ref[...] = _silu(proj(OFF_Z, 1024)).astype(BF16)

    ba = proj(OFF_BA, LANES)
    lane = lax.broadcasted_iota(jnp.int32, ba.shape, 1)
    g = -jnp.exp(alog_ref[...]) * jax.nn.softplus(ba + dtb_ref[...])
    gates = jnp.where(lane < DN_HEADS, jax.nn.sigmoid(ba), jnp.where(lane < 2 * DN_HEADS, g, 0.0))
    gates_ref[...] = gates
    gatest_ref[...] = gates.T[0:2 * DN_HEADS, :]

    cu = proj(OFF_C, 1024) * proj(OFF_U, 1024)
    s_ref[...] = (proj(OFF_B, 1024) * causal_conv(cu, halo_sc_ref, 0, csc_ref, SC_CONV)).astype(BF16)

    ga_ref[...] = jax.nn.sigmoid(proj(OFF_GA, 1024)).astype(BF16)
    gb_ref[...] = jax.nn.sigmoid(proj(OFF_GB, 1024)).astype(BF16)


def _resident(shape):
    return pl.BlockSpec(shape, lambda *_: (0,) * len(shape), pipeline_mode=pl.Buffered(1))


def _inproj(h, n1, w_packed, cq, csc, alog_row, dtb_row, tiles_per_batch):
    rows = h.shape[0]
    n_tiles = rows // TM
    row_spec = pl.BlockSpec((TM, D_MODEL), lambda i: (i, 0))
    act = jax.ShapeDtypeStruct((rows, D_MODEL), BF16)
    return pl.pallas_call(
        functools.partial(_inproj_kernel, tiles_per_batch=tiles_per_batch),
        grid=(n_tiles,),
        in_specs=[
            row_spec,
            _resident((1, D_MODEL)),
            _resident((D_MODEL, PACKED_WIDTH)),
            _resident((DN_CONV, 3 * D_MODEL)),
            _resident((SC_CONV, D_MODEL)),
            _resident((1, LANES)),
            _resident((1, LANES)),
        ],
        out_specs=[
            row_spec, row_spec, row_spec, row_spec,
            pl.BlockSpec((TM, LANES), lambda i: (i, 0)),
            pl.BlockSpec((2 * DN_HEADS, TM), lambda i: (0, i)),
            row_spec, row_spec, row_spec,
        ],
        out_shape=[
            act, act, act, act,
            jax.ShapeDtypeStruct((rows, LANES), F32),
            jax.ShapeDtypeStruct((2 * DN_HEADS, rows), F32),
            act, act, act,
        ],
        scratch_shapes=[
            pltpu.VMEM((SUBLANES + TM, D_MODEL), F32),
            pltpu.VMEM((SUBLANES, 3 * D_MODEL), F32),
            pltpu.VMEM((SUBLANES, D_MODEL), F32),
        ],
        compiler_params=pltpu.CompilerParams(
            dimension_semantics=("arbitrary",), vmem_limit_bytes=VMEM_LIMIT),
        name="inproj",
    )(h, n1, w_packed, cq, csc, alog_row, dtb_row)


def _split3(x):
    x1 = x.astype(BF16)
    r1 = x - x1.astype(F32)
    x2 = r1.astype(BF16)
    x3 = (r1 - x2.astype(F32)).astype(BF16)
    return x1, x2, x3


def _delta_kernel(q_ref, k_ref, v_ref, zs_ref, gates_ref, gatest_ref, onorm_ref, og_ref,
                  state_ref, lhs1_ref, lhs2_ref, u_ref, gl_ref):
    @pl.when(pl.program_id(1) == 0)
    def _():
        state_ref[...] = jnp.zeros_like(state_ref)

    n_chunks = q_ref.shape[0] // CHUNK
    row = lax.broadcasted_iota(jnp.int32, (CHUNK, CHUNK), 0)
    col = lax.broadcasted_iota(jnp.int32, (CHUNK, CHUNK), 1)
    causal = row >= col
    strict = row > col
    same16 = (row // 16) == (col // 16)
    same32 = (row // 32) == (col // 32)
    tril = jnp.where(causal, 1.0, 0.0).astype(BF16)
    triu = jnp.where(row <= col, 1.0, 0.0).astype(BF16)
    eye = jnp.where(row == col, 1.0, 0.0)
    onorm = onorm_ref[...]
    problems = [(c, hd) for c in range(n_chunks) for hd in range(DN_HEADS)]

    def rows_of(c):
        return slice(c * CHUNK, (c + 1) * CHUNK)

    def cols_of(hd):
        return slice(hd * DN_HEAD_DIM, (hd + 1) * DN_HEAD_DIM)

    gates, gcol, grow_all = [], [], []
    for c in range(n_chunks):
        gt = gates_ref[rows_of(c), :]
        g1, g2, g3 = _split3(gt)
        gates.append(gt)
        gcol.append(jnp.dot(tril, g1, preferred_element_type=F32)
                    + jnp.dot(tril, g2, preferred_element_type=F32)
                    + jnp.dot(tril, g3, preferred_element_type=F32))
        t1, t2, t3 = _split3(gatest_ref[c])
        grow_all.append(jnp.dot(t1, triu, preferred_element_type=F32)
                        + jnp.dot(t2, triu, preferred_element_type=F32)
                        + jnp.dot(t3, triu, preferred_element_type=F32))

    decay, kq_lhs, k_bf, rhs = {}, {}, {}, {}
    for p in problems:
        c, hd = p
        q = q_ref[rows_of(c), cols_of(hd)].astype(F32)
        kb16 = k_ref[rows_of(c), cols_of(hd)]
        k = kb16.astype(F32)
        v = v_ref[rows_of(c), cols_of(hd)].astype(F32)
        beta = jnp.broadcast_to(gates[c][:, hd:hd + 1], (CHUNK, DN_HEAD_DIM))
        gc = jnp.broadcast_to(gcol[c][:, DN_HEADS + hd:DN_HEADS + hd + 1], (CHUNK, DN_HEAD_DIM))
        grow = grow_all[c][DN_HEADS + hd:DN_HEADS + hd + 1, :]
        g_last = gc[CHUNK - 1:CHUNK, :]
        decay[p] = jnp.exp(jnp.where(causal, gc[:, 0:CHUNK] - grow, -jnp.inf))
        e_gc = jnp.exp(gc)
        kb = k * beta
        kdec = k * jnp.exp(g_last - gc)
        k_bf[p] = kb16
        kq_lhs[p] = jnp.concatenate([kb, q], axis=0).astype(BF16)
        rhs[p] = jnp.concatenate([v * beta, kb * e_gc], axis=1).astype(BF16)
        lhs1_ref[c, hd, CHUNK:2 * CHUNK, :] = (q * e_gc).astype(BF16)
        lhs2_ref[c, hd, CHUNK:CHUNK + DN_HEAD_DIM, :] = kdec.T.astype(BF16)
        gl_ref[c * DN_HEADS + hd:c * DN_HEADS + hd + 1, :] = jnp.exp(g_last)

    def level(fn):
        return {p: fn(p) for p in problems}

    kq = level(lambda p: lax.dot_general(kq_lhs[p], k_bf[p], (((1,), (1,)), ((), ())),
                                         preferred_element_type=F32))
    a_mat = level(lambda p: jnp.where(strict, kq[p][0:CHUNK] * decay[p], 0.0))
    for p in problems:
        lhs2_ref[p[0], p[1], 0:CHUNK, :] = (kq[p][CHUNK:2 * CHUNK] * decay[p]).astype(BF16)

    n1 = level(lambda p: jnp.where(same16, -a_mat[p], 0.0).astype(BF16))
    n2 = level(lambda p: jnp.dot(n1[p], n1[p], preferred_element_type=F32).astype(BF16))
    t = level(lambda p: eye + n1[p].astype(F32))
    n4 = level(lambda p: jnp.dot(n2[p], n2[p], preferred_element_type=F32).astype(BF16))
    t = level(lambda p: t[p] + _bdot(t[p], n2[p]))
    n8 = level(lambda p: jnp.dot(n4[p], n4[p], preferred_element_type=F32).astype(BF16))
    t = level(lambda p: t[p] + _bdot(t[p], n4[p]))
    t = level(lambda p: t[p] + _bdot(t[p], n8[p]))
    a1 = level(lambda p: jnp.where(same32 & jnp.logical_not(same16), a_mat[p], 0.0).astype(BF16))
    m = level(lambda p: _bdot(a1[p], t[p]))
    t = level(lambda p: t[p] - _bdot(t[p], m[p]))
    a2 = level(lambda p: jnp.where(same32, 0.0, a_mat[p]).astype(BF16))
    m = level(lambda p: _bdot(a2[p], t[p]))
    t = level(lambda p: t[p] - _bdot(t[p], m[p]))
    uw = level(lambda p: _bdot(t[p], rhs[p]))
    for p in problems:
        u_ref[p[0], p[1]] = uw[p][:, 0:DN_HEAD_DIM]
        lhs1_ref[p[0], p[1], 0:CHUNK, :] = uw[p][:, DN_HEAD_DIM:2 * DN_HEAD_DIM].astype(BF16)

    heads = range(DN_HEADS)
    state = [state_ref[hd] for hd in heads]
    for c in range(n_chunks):
        state_bf = [state[hd].astype(BF16) for hd in heads]
        r1 = [jnp.dot(lhs1_ref[c, hd], state_bf[hd], preferred_element_type=F32) for hd in heads]
        v_new = [(u_ref[c, hd] - r1[hd][0:CHUNK]).astype(BF16) for hd in heads]
        r2 = [jnp.dot(lhs2_ref[c, hd], v_new[hd], preferred_element_type=F32) for hd in heads]
        for hd in heads:
            gl = gl_ref[c * DN_HEADS + hd:c * DN_HEADS + hd + 1, :]
            state[hd] = state[hd] * gl + r2[hd][CHUNK:CHUNK + DN_HEAD_DIM]
            o = r1[hd][CHUNK:2 * CHUNK] + r2[hd][0:CHUNK]
            o = o * lax.rsqrt(jnp.mean(o * o, axis=-1, keepdims=True) + EPS) * onorm
            og_ref[rows_of(c), cols_of(hd)] = (
                o * zs_ref[rows_of(c), cols_of(hd)].astype(F32)).astype(BF16)
    for hd in heads:
        state_ref[hd] = state[hd]


def _delta(q, k, v, zs, gates, gatest_chunks, onorm, bsz):
    rows = q.shape[0]
    steps = rows // bsz // TC
    n_chunks = TC // CHUNK
    row_spec = pl.BlockSpec((TC, D_MODEL), lambda b, i: (b * steps + i, 0))
    return pl.pallas_call(
        _delta_kernel,
        grid=(bsz, steps),
        in_specs=[
            row_spec, row_spec, row_spec, row_spec,
            pl.BlockSpec((TC, LANES), lambda b, i: (b * steps + i, 0)),
            pl.BlockSpec((n_chunks, 2 * DN_HEADS, CHUNK), lambda b, i: (b * steps + i, 0, 0)),
            pl.BlockSpec((1, DN_HEAD_DIM), lambda b, i: (0, 0)),
        ],
        out_specs=row_spec,
        out_shape=jax.ShapeDtypeStruct((rows, D_MODEL), BF16),
        scratch_shapes=[
            pltpu.VMEM((DN_HEADS, DN_HEAD_DIM, DN_HEAD_DIM), F32),
            pltpu.VMEM((n_chunks, DN_HEADS, 2 * CHUNK, DN_HEAD_DIM), BF16),
            pltpu.VMEM((n_chunks, DN_HEADS, CHUNK + DN_HEAD_DIM, CHUNK), BF16),
            pltpu.VMEM((n_chunks, DN_HEADS, CHUNK, DN_HEAD_DIM), F32),
            pltpu.VMEM((n_chunks * DN_HEADS, DN_HEAD_DIM), F32),
        ],
        compiler_params=pltpu.CompilerParams(
            dimension_semantics=("arbitrary", "arbitrary"), vmem_limit_bytes=VMEM_LIMIT),
        name="delta_rule",
    )(q, k, v, zs, gates, gatest_chunks, onorm)


FF_SPLITS = ((0, 1024), (1024, 2048), (2048, D_FF))


def _ffn_kernel(h_ref, og_ref, s_ref, ga_ref, gb_ref, wdn_ref, wsc_ref, wo_ref, n2_ref,
                wgu_ref, wdown_ref, fnorm_ref, out_ref, *, final):
    ya = jnp.dot(og_ref[...], wdn_ref[...], preferred_element_type=F32)
    yb = jnp.dot(s_ref[...], wsc_ref[...], preferred_element_type=F32)
    mixed = ga_ref[...].astype(F32) * ya + gb_ref[...].astype(F32) * yb
    h1 = h_ref[...] + _bdot(mixed, wo_ref[...])
    hn = h1 * lax.rsqrt(jnp.mean(h1 * h1, axis=-1, keepdims=True) + EPS) * n2_ref[...]
    hn = hn.astype(BF16)
    acc = h1
    for lo, hi in FF_SPLITS:
        gate = jnp.dot(hn, wgu_ref[:, lo:hi], preferred_element_type=F32)
        up = jnp.dot(hn, wgu_ref[:, D_FF + lo:D_FF + hi], preferred_element_type=F32)
        acc = acc + _bdot(_silu(gate) * up, wdown_ref[lo:hi, :])
    if final:
        acc = acc * lax.rsqrt(jnp.mean(acc * acc, axis=-1, keepdims=True) + EPS) * fnorm_ref[...]
    out_ref[...] = acc


def _ffn(h, og, s, ga, gb, wdn, wsc, wo, n2, wgu, wdown, fnorm, final):
    rows = h.shape[0]
    row_spec = pl.BlockSpec((TM, D_MODEL), lambda i: (i, 0))
    return pl.pallas_call(
        functools.partial(_ffn_kernel, final=final),
        grid=(rows // TM,),
        in_specs=[
            row_spec, row_spec, row_spec, row_spec, row_spec,
            _resident((D_MODEL, D_MODEL)), _resident((D_MODEL, D_MODEL)), _resident((D_MODEL, D_MODEL)),
            _resident((1, D_MODEL)),
            _resident((D_MODEL, 2 * D_FF)), _resident((D_FF, D_MODEL)),
            _resident((1, D_MODEL)),
        ],
        out_specs=row_spec,
        out_shape=jax.ShapeDtypeStruct((rows, D_MODEL), F32),
        compiler_params=pltpu.CompilerParams(
            dimension_semantics=("arbitrary",), vmem_limit_bytes=VMEM_LIMIT),
        name="mix_ffn",
    )(h, og, s, ga, gb, wdn, wsc, wo, n2, wgu, wdown, fnorm)


def _pack_w_in(w):
    ba = jnp.pad(w[:, 4096:4112], ((0, 0), (0, LANES - 2 * DN_HEADS)))
    return jnp.concatenate([w[:, 0:4096], ba, w[:, 4112:]], axis=1).astype(BF16)


def _gate_row(p):
    return jnp.pad(p.astype(F32), (DN_HEADS, LANES - 2 * DN_HEADS)).reshape(1, LANES)


def kernel(x, meta_tokens, norm1, w_in, conv_qkv, a_log, dt_bias, o_norm, w_dn_out, conv_sc,
           w_sc_out, w_o, norm2, w_gate_up, w_down, final_norm):
    bsz, seq, d = x.shape
    depth = w_in.shape[0]
    assert d == D_MODEL and seq % TM == 0 and TM % TC == 0 and TM >= N_META
    lp = TM + seq
    tiles_per_batch = lp // TM

    front = jnp.zeros((bsz, TM - N_META, d), x.dtype)
    meta = jnp.broadcast_to(meta_tokens[None].astype(x.dtype), (bsz, N_META, d))
    h = jnp.concatenate([front, meta, x], axis=1).reshape(bsz * lp, d)

    for layer in range(depth):
        q, k, v, zs, gates, gatest, s, ga, gb = _inproj(
            h, norm1[layer].reshape(1, d), _pack_w_in(w_in[layer]), conv_qkv[layer], conv_sc[layer],
            _gate_row(a_log[layer]), _gate_row(dt_bias[layer]), tiles_per_batch)
        gatest_chunks = gatest.reshape(2 * DN_HEADS, -1, CHUNK).transpose(1, 0, 2)
        og = _delta(q, k, v, zs, gates, gatest_chunks, o_norm[layer].reshape(1, DN_HEAD_DIM), bsz)
        h = _ffn(h, og, s, ga, gb, w_dn_out[layer].astype(BF16), w_sc_out[layer].astype(BF16),
                 w_o[layer].astype(BF16), norm2[layer].reshape(1, d), w_gate_up[layer].astype(BF16),
                 w_down[layer].astype(BF16), final_norm.reshape(1, d), layer == depth - 1)

    return h.reshape(bsz, lp, d)[:, TM:]
```

```python
import functools

import jax
import jax.numpy as jnp
from jax import lax
from jax.experimental import pallas as pl
from jax.experimental.pallas import tpu as pltpu

D_MODEL = 1024
N_META = 16
DN_HEADS = 8
DN_HEAD_DIM = 128
DN_CONV = 4
SC_CONV = 3
CHUNK = 64
D_FF = 2816
EPS = 1e-6

LANES = 128
SUBLANES = 8
TM = 256
TC = 256
VMEM_LIMIT = 56 * 1024 * 1024

W_IN_GATES = slice(4 * D_MODEL, 4 * D_MODEL + 2 * DN_HEADS)
OFF_Q, OFF_K, OFF_V, OFF_Z = 0, 1024, 2048, 3072
OFF_BA = 4096
OFF_C, OFF_U, OFF_B, OFF_GA, OFF_GB = 4224, 5248, 6272, 7296, 8320
PACKED_WIDTH = 9344

F32 = jnp.float32
BF16 = jnp.bfloat16


def _bdot(a, b):
    return jnp.dot(a.astype(BF16), b.astype(BF16), preferred_element_type=F32)


def _silu(x):
    return x * jax.nn.sigmoid(x)


def _rms(x, gain):
    return x * lax.rsqrt(jnp.mean(x * x, axis=-1, keepdims=True) + EPS) * gain


def _layer_block(layer, shape):
    return pl.BlockSpec((None,) + shape, lambda *_: (layer,) + (0,) * len(shape),
                        pipeline_mode=pl.Buffered(1))


def _resident(shape):
    return pl.BlockSpec(shape, lambda *_: (0,) * len(shape), pipeline_mode=pl.Buffered(1))


def _input_tile(x_ref, meta_ref, tile_in_batch):
    tm, d = x_ref.shape
    front = jnp.concatenate([jnp.zeros((tm - N_META, d), F32), meta_ref[...]], axis=0)
    return jnp.where(tile_in_batch == 0, front, x_ref[...])


def _hidden_specs(first, tiles_per_batch, tile_of=lambda i: i):
    if not first:
        return [pl.BlockSpec((TM, D_MODEL), lambda i: (tile_of(i), 0))]
    seq_tiles = tiles_per_batch - 1

    def x_map(i):
        t = tile_of(i)
        return ((t // tiles_per_batch) * seq_tiles + jnp.maximum(t % tiles_per_batch - 1, 0), 0)

    return [pl.BlockSpec((TM, D_MODEL), x_map), _resident((N_META, D_MODEL))]


def _inproj_kernel(*refs, tiles_per_batch, first):
    n_hidden = 2 if first else 1
    hidden = refs[:n_hidden]
    (n1_ref, w_ref, cq_ref, csc_ref, alog_ref, dtb_ref,
     qkv_ref, zs_ref, gates_ref, gatest_ref, sgg_ref,
     halo_qkv_ref, halo_sc_ref) = refs[n_hidden:]
    tile_in_batch = pl.program_id(0) % tiles_per_batch
    tm = qkv_ref.shape[0]

    @pl.when(tile_in_batch == 0)
    def _():
        halo_qkv_ref[...] = jnp.zeros_like(halo_qkv_ref)
        halo_sc_ref[...] = jnp.zeros_like(halo_sc_ref)

    x = _input_tile(*hidden, tile_in_batch) if first else hidden[0][...]
    xn = _rms(x, n1_ref[...]).astype(BF16)

    def proj(off, width=D_MODEL):
        return jnp.dot(xn, w_ref[:, off:off + width], preferred_element_type=F32)

    def causal_conv(raw, halo_ref, col, taps_ref, n_taps):
        width = raw.shape[1]
        ext = jnp.concatenate([halo_ref[:, col:col + width], raw], axis=0)
        halo_ref[:, col:col + width] = raw[tm - SUBLANES:tm, :]
        acc = raw * taps_ref[n_taps - 1:n_taps, col:col + width]
        for j in range(n_taps - 1):
            start = SUBLANES - (n_taps - 1) + j
            acc = acc + ext[start:start + tm, :] * taps_ref[j:j + 1, col:col + width]
        return acc

    def head_l2norm(y, scale):
        outs = []
        for hd in range(y.shape[1] // DN_HEAD_DIM):
            yh = y[:, hd * DN_HEAD_DIM:(hd + 1) * DN_HEAD_DIM]
            inv = lax.rsqrt(jnp.sum(yh * yh, axis=-1, keepdims=True) + EPS)
            outs.append(yh * (inv * scale))
        return jnp.concatenate(outs, axis=-1)

    yq = _silu(causal_conv(proj(OFF_Q), halo_qkv_ref, 0, cq_ref, DN_CONV))
    qkv_ref[:, 0:D_MODEL] = head_l2norm(yq, DN_HEAD_DIM ** -0.5).astype(BF16)
    yk = _silu(causal_conv(proj(OFF_K), halo_qkv_ref, D_MODEL, cq_ref, DN_CONV))
    qkv_ref[:, D_MODEL:2 * D_MODEL] = head_l2norm(yk, 1.0).astype(BF16)
    yv = _silu(causal_conv(proj(OFF_V), halo_qkv_ref, 2 * D_MODEL, cq_ref, DN_CONV))
    qkv_ref[:, 2 * D_MODEL:3 * D_MODEL] = yv.astype(BF16)
    zs_ref[...] = _silu(proj(OFF_Z)).astype(BF16)

    cu = proj(OFF_C) * proj(OFF_U)
    sgg_ref[:, 0:D_MODEL] = (
        proj(OFF_B) * causal_conv(cu, halo_sc_ref, 0, csc_ref, SC_CONV)).astype(BF16)
    sgg_ref[:, D_MODEL:2 * D_MODEL] = jax.nn.sigmoid(proj(OFF_GA)).astype(BF16)
    sgg_ref[:, 2 * D_MODEL:3 * D_MODEL] = jax.nn.sigmoid(proj(OFF_GB)).astype(BF16)

    ba = proj(OFF_BA, LANES)
    lane = lax.broadcasted_iota(jnp.int32, ba.shape, 1)
    g = -jnp.exp(alog_ref[...]) * jax.nn.softplus(ba + dtb_ref[...])
    gates = jnp.where(lane < DN_HEADS, jax.nn.sigmoid(ba), jnp.where(lane < 2 * DN_HEADS, g, 0.0))
    gates_ref[...] = gates
    gates_t = gates.T
    for c in range(tm // CHUNK):
        gatest_ref[c] = gates_t[0:2 * DN_HEADS, c * CHUNK:(c + 1) * CHUNK]


def _inproj(hidden, layer, n1, w_packed, cq, csc, alog_rows, dtb_rows, rows, tiles_per_batch):
    first = len(hidden) == 2

    def rows_of(width):
        return pl.BlockSpec((TM, width), lambda i: (i, 0))

    return pl.pallas_call(
        functools.partial(_inproj_kernel, tiles_per_batch=tiles_per_batch, first=first),
        grid=(rows // TM,),
        in_specs=_hidden_specs(first, tiles_per_batch) + [
            _layer_block(layer, (1, D_MODEL)),
            _layer_block(layer, (D_MODEL, PACKED_WIDTH)),
            _layer_block(layer, (DN_CONV, 3 * D_MODEL)),
            _layer_block(layer, (SC_CONV, D_MODEL)),
            _layer_block(layer, (1, LANES)),
            _layer_block(layer, (1, LANES)),
        ],
        out_specs=[
            rows_of(3 * D_MODEL), rows_of(D_MODEL), rows_of(LANES),
            pl.BlockSpec((TM // CHUNK, 2 * DN_HEADS, CHUNK), lambda i: (i, 0, 0)),
            rows_of(3 * D_MODEL),
        ],
        out_shape=[
            jax.ShapeDtypeStruct((rows, 3 * D_MODEL), BF16),
            jax.ShapeDtypeStruct((rows, D_MODEL), BF16),
            jax.ShapeDtypeStruct((rows, LANES), F32),
            jax.ShapeDtypeStruct((rows // CHUNK, 2 * DN_HEADS, CHUNK), F32),
            jax.ShapeDtypeStruct((rows, 3 * D_MODEL), BF16),
        ],
        scratch_shapes=[
            pltpu.VMEM((SUBLANES, 3 * D_MODEL), F32),
            pltpu.VMEM((SUBLANES, D_MODEL), F32),
        ],
        compiler_params=pltpu.CompilerParams(
            dimension_semantics=("arbitrary",), vmem_limit_bytes=VMEM_LIMIT),
        name="inproj",
    )(*hidden, n1, w_packed, cq, csc, alog_rows, dtb_rows)


def _split3(x):
    x1 = x.astype(BF16)
    r1 = x - x1.astype(F32)
    x2 = r1.astype(BF16)
    x3 = (r1 - x2.astype(F32)).astype(BF16)
    return x1, x2, x3


def _delta_kernel(q_ref, k_ref, v_ref, zs_ref, gates_ref, gatest_ref, onorm_ref, og_ref,
                  state_ref, lhs1_ref, lhs2_ref, u_ref, gl_ref):
    @pl.when(pl.program_id(0) == 0)
    def _():
        state_ref[...] = jnp.zeros_like(state_ref)

    bsz, tc, _ = q_ref.shape
    n_chunks = tc // CHUNK
    n_slots = bsz * n_chunks
    row = lax.broadcasted_iota(jnp.int32, (CHUNK, 2 * CHUNK), 0)
    lane = lax.broadcasted_iota(jnp.int32, (CHUNK, 2 * CHUNK), 1)
    col = lane & (CHUNK - 1)
    left = lane < CHUNK
    causal = row >= col
    strict = row > col
    same16 = (row // 16) == (col // 16)
    same32 = (row // 32) == (col // 32)
    eye = jnp.where(row == col, 1.0, 0.0)
    tril = jnp.where(causal[:, 0:CHUNK], 1.0, 0.0).astype(BF16)
    triu_l = jnp.where(left & (row <= col), 1.0, 0.0).astype(BF16)
    triu_r = jnp.where(jnp.logical_not(left) & (row <= col), 1.0, 0.0).astype(BF16)
    zeros_bf = jnp.zeros((CHUNK, DN_HEAD_DIM), BF16)
    zeros_f = jnp.zeros((CHUNK, DN_HEAD_DIM), F32)
    onorm = onorm_ref[...]
    problems = [(c, hd) for c in range(n_slots) for hd in range(DN_HEADS)]
    pairs = [(c, j) for c in range(n_slots) for j in range(DN_HEADS // 2)]
    nt_dims = (((1,), (1,)), ((), ()))

    def tokens(slot):
        return slot // n_chunks, slice((slot % n_chunks) * CHUNK, (slot % n_chunks + 1) * CHUNK)

    def head_tile(slot, hd):
        return tokens(slot) + (slice(hd * DN_HEAD_DIM, (hd + 1) * DN_HEAD_DIM),)

    def dot3(a_parts, b):
        return sum(jnp.dot(a, b, preferred_element_type=F32) for a in a_parts)

    def half_pad(x, hd, zeros):
        return jnp.concatenate([x, zeros] if hd % 2 == 0 else [zeros, x], axis=0)

    def blockdiag(y):
        return jnp.concatenate([jnp.where(left, y, 0.0), jnp.where(left, 0.0, y)], axis=0).astype(BF16)

    gates, gcol, grow_l, grow_r = [], [], [], []
    for c in range(n_slots):
        gt = gates_ref[tokens(c) + (slice(None),)]
        gates.append(gt)
        g_parts = _split3(gt)
        gcol.append(sum(jnp.dot(tril, gp, preferred_element_type=F32) for gp in g_parts))
        t_parts = _split3(gatest_ref[c // n_chunks, c % n_chunks])
        grow_l.append(dot3(t_parts, triu_l))
        grow_r.append(dot3(t_parts, triu_r))

    gc_b, kq_lhs, k_bf, rhs = {}, {}, {}, {}
    for p in problems:
        c, hd = p
        q = q_ref[head_tile(c, hd)].astype(F32)
        k_bf[p] = k_ref[head_tile(c, hd)]
        k = k_bf[p].astype(F32)
        v = v_ref[head_tile(c, hd)].astype(F32)
        beta = jnp.broadcast_to(gates[c][:, hd:hd + 1], (CHUNK, DN_HEAD_DIM))
        gc = jnp.broadcast_to(gcol[c][:, DN_HEADS + hd:DN_HEADS + hd + 1], (CHUNK, DN_HEAD_DIM))
        gc_b[p] = gc
        g_last = gc[CHUNK - 1:CHUNK, :]
        e_gc = jnp.exp(gc)
        kb = k * beta
        kdec = k * jnp.exp(g_last - gc)
        kq_lhs[p] = jnp.concatenate([kb, q], axis=0).astype(BF16)
        rhs[p] = half_pad(jnp.concatenate([v * beta, kb * e_gc], axis=1).astype(BF16), hd,
                          jnp.zeros((CHUNK, 2 * DN_HEAD_DIM), BF16))
        lhs1_ref[c, hd, CHUNK:2 * CHUNK, :] = (q * e_gc).astype(BF16)
        lhs2_ref[c, hd, CHUNK:CHUNK + DN_HEAD_DIM, :] = half_pad(kdec, hd, zeros_f).T.astype(BF16)
        gl_ref[c * DN_HEADS + hd:c * DN_HEADS + hd + 1, :] = jnp.exp(g_last)

    def level(fn):
        return {p: fn(p) for p in pairs}

    def heads_of(p):
        return (p[0], 2 * p[1]), (p[0], 2 * p[1] + 1)

    def pair_decay(p):
        h0, h1 = heads_of(p)
        c = p[0]
        grow = (grow_l[c][DN_HEADS + h0[1]:DN_HEADS + h0[1] + 1, :]
                + grow_r[c][DN_HEADS + h1[1]:DN_HEADS + h1[1] + 1, :])
        gc = jnp.where(left, gc_b[h0], gc_b[h1])
        return jnp.exp(jnp.where(causal, gc - grow, -jnp.inf))

    def pair_kq(p):
        h0, h1 = heads_of(p)
        keys = jnp.concatenate([k_bf[h0], k_bf[h1]], axis=0)
        both = lax.dot_general(jnp.concatenate([kq_lhs[h0], kq_lhs[h1]], axis=0), keys, nt_dims,
                               preferred_element_type=F32)
        return jnp.where(jnp.concatenate([left, left], axis=0),
                         both[0:2 * CHUNK], both[2 * CHUNK:4 * CHUNK])

    decay = level(pair_decay)
    kq = level(pair_kq)
    a_mat = level(lambda p: jnp.where(strict, kq[p][0:CHUNK] * decay[p], 0.0))
    for p in pairs:
        qk = (kq[p][CHUNK:2 * CHUNK] * decay[p]).astype(BF16)
        for c, hd in heads_of(p):
            lhs2_ref[c, hd, 0:CHUNK, :] = qk

    def pdot(x, y_bd):
        return jnp.dot(x.astype(BF16), y_bd, preferred_element_type=F32)

    n1 = level(lambda p: jnp.where(same16, -a_mat[p], 0.0))
    n2 = level(lambda p: pdot(n1[p], blockdiag(n1[p])))
    t = level(lambda p: eye + n1[p])
    tn = level(lambda p: pdot(jnp.concatenate([t[p], n2[p]], axis=0), blockdiag(n2[p])))
    t = level(lambda p: t[p] + tn[p][0:CHUNK])
    n4 = level(lambda p: tn[p][CHUNK:2 * CHUNK])
    tn = level(lambda p: pdot(jnp.concatenate([t[p], n4[p]], axis=0), blockdiag(n4[p])))
    t = level(lambda p: t[p] + tn[p][0:CHUNK])
    t = level(lambda p: t[p] + pdot(t[p], blockdiag(tn[p][CHUNK:2 * CHUNK])))
    a1 = level(lambda p: jnp.where(same32 & jnp.logical_not(same16), a_mat[p], 0.0))
    m = level(lambda p: pdot(a1[p], blockdiag(t[p])))
    t = level(lambda p: t[p] - pdot(t[p], blockdiag(m[p])))
    a2 = level(lambda p: jnp.where(same32, 0.0, a_mat[p]))
    m = level(lambda p: pdot(a2[p], blockdiag(t[p])))
    t = level(lambda p: (t[p] - pdot(t[p], blockdiag(m[p]))).astype(BF16))
    for p in pairs:
        for c, hd in heads_of(p):
            uw = jnp.dot(t[p], rhs[(c, hd)], preferred_element_type=F32)
            u_ref[c, hd] = uw[:, 0:DN_HEAD_DIM]
            lhs1_ref[c, hd, 0:CHUNK, :] = uw[:, DN_HEAD_DIM:2 * DN_HEAD_DIM].astype(BF16)

    chains = [(b, hd) for b in range(bsz) for hd in range(DN_HEADS)]
    state = {ch: state_ref[ch[0] * DN_HEADS + ch[1]] for ch in chains}
    for c in range(n_chunks):
        r1, r2 = {}, {}
        for b, hd in chains:
            r1[b, hd] = jnp.dot(lhs1_ref[b * n_chunks + c, hd], state[b, hd].astype(BF16),
                                preferred_element_type=F32)
        for b, hd in chains:
            slot = b * n_chunks + c
            v_new = half_pad((u_ref[slot, hd] - r1[b, hd][0:CHUNK]).astype(BF16), hd, zeros_bf)
            r2[b, hd] = jnp.dot(lhs2_ref[slot, hd], v_new, preferred_element_type=F32)
        for b, hd in chains:
            slot = b * n_chunks + c
            gl = gl_ref[slot * DN_HEADS + hd:slot * DN_HEADS + hd + 1, :]
            state[b, hd] = state[b, hd] * gl + r2[b, hd][CHUNK:CHUNK + DN_HEAD_DIM]
            o = r1[b, hd][CHUNK:2 * CHUNK] + r2[b, hd][0:CHUNK]
            og_ref[head_tile(slot, hd)] = (
                _rms(o, onorm) * zs_ref[head_tile(slot, hd)].astype(F32)).astype(BF16)
    for b, hd in chains:
        state_ref[b * DN_HEADS + hd] = state[b, hd]


def _delta(qkv, zs, gates, gatest, layer, onorm, bsz):
    rows = qkv.shape[0]
    lp = rows // bsz
    n_chunks = TC // CHUNK
    n_slots = bsz * n_chunks

    def per_batch(a):
        return a.reshape((bsz, a.shape[0] // bsz) + a.shape[1:])

    def column_block(j):
        return pl.BlockSpec((bsz, TC, D_MODEL), lambda i: (0, i, j))

    row_spec = column_block(0)
    qkv = per_batch(qkv)
    og = pl.pallas_call(
        _delta_kernel,
        grid=(lp // TC,),
        in_specs=[
            column_block(0), column_block(1), column_block(2), row_spec,
            pl.BlockSpec((bsz, TC, LANES), lambda i: (0, i, 0)),
            pl.BlockSpec((bsz, n_chunks, 2 * DN_HEADS, CHUNK), lambda i: (0, i, 0, 0)),
            _layer_block(layer, (1, DN_HEAD_DIM)),
        ],
        out_specs=row_spec,
        out_shape=jax.ShapeDtypeStruct((bsz, lp, D_MODEL), BF16),
        scratch_shapes=[
            pltpu.VMEM((bsz * DN_HEADS, DN_HEAD_DIM, DN_HEAD_DIM), F32),
            pltpu.VMEM((n_slots, DN_HEADS, 2 * CHUNK, DN_HEAD_DIM), BF16),
            pltpu.VMEM((n_slots, DN_HEADS, CHUNK + DN_HEAD_DIM, 2 * CHUNK), BF16),
            pltpu.VMEM((n_slots, DN_HEADS, CHUNK, DN_HEAD_DIM), F32),
            pltpu.VMEM((n_slots * DN_HEADS, DN_HEAD_DIM), F32),
        ],
        compiler_params=pltpu.CompilerParams(
            dimension_semantics=("arbitrary",), vmem_limit_bytes=VMEM_LIMIT),
        name="delta_rule",
    )(qkv, qkv, qkv, *(per_batch(a) for a in (zs, gates, gatest)), onorm)
    return og.reshape(rows, D_MODEL)


FF_SPLITS = ((0, 1024), (1024, 2048), (2048, D_FF))


def _ffn_kernel(*refs, tiles_per_batch, first, final):
    n_hidden = 2 if first else 1
    hidden = refs[:n_hidden]
    (og_ref, s_ref, ga_ref, gb_ref, wdn_ref, wsc_ref, wo_ref, n2_ref, wgu_ref, wdown_ref,
     fnorm_ref, out_ref) = refs[n_hidden:]
    tile_in_batch = pl.program_id(0) % tiles_per_batch

    def body():
        h = _input_tile(*hidden, tile_in_batch) if first else hidden[0][...]
        ya = jnp.dot(og_ref[...], wdn_ref[...], preferred_element_type=F32)
        yb = jnp.dot(s_ref[...], wsc_ref[...], preferred_element_type=F32)
        mixed = ga_ref[...].astype(F32) * ya + gb_ref[...].astype(F32) * yb
        h1 = h + _bdot(mixed, wo_ref[...])
        hn = _rms(h1, n2_ref[...]).astype(BF16)
        acc = h1
        for lo, hi in FF_SPLITS:
            gate = jnp.dot(hn, wgu_ref[:, lo:hi], preferred_element_type=F32)
            up = jnp.dot(hn, wgu_ref[:, D_FF + lo:D_FF + hi], preferred_element_type=F32)
            acc = acc + _bdot(_silu(gate) * up, wdown_ref[lo:hi, :])
        out_ref[...] = _rms(acc, fnorm_ref[...]) if final else acc

    if final:
        pl.when(tile_in_batch != 0)(body)
    else:
        body()


def _ffn(hidden, og, sgg, layer, wdn, wsc, wo, n2, wgu, wdown, fnorm, rows, tiles_per_batch,
         final):
    first = len(hidden) == 2
    row_spec = pl.BlockSpec((TM, D_MODEL), lambda i: (i, 0))

    def sgg_block(j):
        return pl.BlockSpec((TM, D_MODEL), lambda i: (i, j))

    if final:
        seq_tiles = tiles_per_batch - 1
        out_rows = rows // tiles_per_batch * seq_tiles
        out_spec = pl.BlockSpec(
            (TM, D_MODEL),
            lambda i: ((i // tiles_per_batch) * seq_tiles + jnp.maximum(i % tiles_per_batch - 1, 0), 0))
    else:
        out_rows, out_spec = rows, row_spec
    return pl.pallas_call(
        functools.partial(_ffn_kernel, tiles_per_batch=tiles_per_batch, first=first, final=final),
        grid=(rows // TM,),
        in_specs=_hidden_specs(first, tiles_per_batch) + [
            row_spec, sgg_block(0), sgg_block(1), sgg_block(2),
            _layer_block(layer, (D_MODEL, D_MODEL)),
            _layer_block(layer, (D_MODEL, D_MODEL)),
            _layer_block(layer, (D_MODEL, D_MODEL)),
            _layer_block(layer, (1, D_MODEL)),
            _layer_block(layer, (D_MODEL, 2 * D_FF)),
            _layer_block(layer, (D_FF, D_MODEL)),
            _resident((1, D_MODEL)),
        ],
        out_specs=out_spec,
        out_shape=jax.ShapeDtypeStruct((out_rows, D_MODEL), F32),
        compiler_params=pltpu.CompilerParams(
            dimension_semantics=("arbitrary",), vmem_limit_bytes=VMEM_LIMIT),
        name="mix_ffn",
    )(*hidden, og, sgg, sgg, sgg, wdn, wsc, wo, n2, wgu, wdown, fnorm)


def _pack_kernel(wt_ref, out_ref):
    g0, g1 = W_IN_GATES.start, W_IN_GATES.stop
    out_ref[:, 0:g0] = wt_ref[0:g0, :].T.astype(BF16)
    gates = wt_ref[g0:g0 + LANES, :].T
    lane = lax.broadcasted_iota(jnp.int32, gates.shape, 1)
    out_ref[:, g0:g0 + LANES] = jnp.where(lane < g1 - g0, gates, 0.0).astype(BF16)
    out_ref[:, g0 + LANES:] = wt_ref[g1:, :].T.astype(BF16)


def _pack_w_in(w_in):
    depth, d, width = w_in.shape
    return pl.pallas_call(
        _pack_kernel,
        grid=(depth, d // LANES),
        in_specs=[pl.BlockSpec((None, width, LANES), lambda l, i: (l, 0, i))],
        out_specs=pl.BlockSpec((None, LANES, PACKED_WIDTH), lambda l, i: (l, i, 0)),
        out_shape=jax.ShapeDtypeStruct((depth, d, PACKED_WIDTH), BF16),
        compiler_params=pltpu.CompilerParams(
            dimension_semantics=("arbitrary", "arbitrary"), vmem_limit_bytes=VMEM_LIMIT),
        name="pack_w_in",
    )(jnp.swapaxes(w_in, 1, 2))


def _gate_rows(p):
    return jnp.pad(p.astype(F32), ((0, 0), (DN_HEADS, LANES - 2 * DN_HEADS)))[:, None, :]


def kernel(x, meta_tokens, norm1, w_in, conv_qkv, a_log, dt_bias, o_norm, w_dn_out, conv_sc,
           w_sc_out, w_o, norm2, w_gate_up, w_down, final_norm):
    bsz, seq, d = x.shape
    depth = w_in.shape[0]
    assert d == D_MODEL and seq % TM == 0 and TM % TC == 0 and TM >= N_META
    tiles_per_batch = seq // TM + 1
    rows = bsz * tiles_per_batch * TM

    w_packed = _pack_w_in(w_in)
    wdn, wsc, wo = w_dn_out.astype(BF16), w_sc_out.astype(BF16), w_o.astype(BF16)
    wgu, wdown = w_gate_up.astype(BF16), w_down.astype(BF16)
    n1, n2, onorm = norm1[:, None, :], norm2[:, None, :], o_norm[:, None, :]
    alog_rows, dtb_rows = _gate_rows(a_log), _gate_rows(dt_bias)
    fnorm = final_norm.reshape(1, d)

    hidden = (x.reshape(bsz * seq, d), meta_tokens.astype(x.dtype))
    for layer in range(depth):
        qkv, zs, gates, gatest, sgg = _inproj(
            hidden, layer, n1, w_packed, conv_qkv, conv_sc, alog_rows, dtb_rows,
            rows, tiles_per_batch)
        og = _delta(qkv, zs, gates, gatest, layer, onorm, bsz)
        h = _ffn(hidden, og, sgg, layer, wdn, wsc, wo, n2, wgu, wdown, fnorm,
                 rows, tiles_per_batch, layer == depth - 1)
        hidden = (h,)
    return hidden[0].reshape(bsz, seq, d)
```

```python
import functools

import jax
import jax.numpy as jnp
from jax import lax
from jax.experimental import pallas as pl
from jax.experimental.pallas import tpu as pltpu

D_MODEL = 1024
N_META = 16
DN_HEADS = 8
DN_HEAD_DIM = 128
DN_CONV = 4
SC_CONV = 3
CHUNK = 64
D_FF = 2816
EPS = 1e-6

LANES = 128
SUBLANES = 8
TM = 256
TC = 256
VMEM_LIMIT = 56 * 1024 * 1024

W_IN_GATES = slice(4 * D_MODEL, 4 * D_MODEL + 2 * DN_HEADS)
OFF_Q, OFF_K, OFF_V, OFF_Z = 0, 1024, 2048, 3072
OFF_BA = 4096
OFF_C, OFF_U, OFF_B, OFF_GA, OFF_GB = 4224, 5248, 6272, 7296, 8320
PACKED_WIDTH = 9344

F32 = jnp.float32
BF16 = jnp.bfloat16


def _bdot(a, b):
    return jnp.dot(a.astype(BF16), b.astype(BF16), preferred_element_type=F32)


def _silu(x):
    return x * jax.nn.sigmoid(x)


def _rms(x, gain):
    return x * lax.rsqrt(jnp.mean(x * x, axis=-1, keepdims=True) + EPS) * gain


def _layer_block(layer, shape):
    return pl.BlockSpec((None,) + shape, lambda *_: (layer,) + (0,) * len(shape),
                        pipeline_mode=pl.Buffered(1))


def _resident(shape):
    return pl.BlockSpec(shape, lambda *_: (0,) * len(shape), pipeline_mode=pl.Buffered(1))


def _input_tile(x_ref, meta_ref, tile_in_batch):
    tm, d = x_ref.shape
    front = jnp.concatenate([jnp.zeros((tm - N_META, d), F32), meta_ref[...]], axis=0)
    return jnp.where(tile_in_batch == 0, front, x_ref[...])


def _hidden_specs(first, tiles_per_batch):
    if not first:
        return [pl.BlockSpec((TM, D_MODEL), lambda i: (i, 0))]
    seq_tiles = tiles_per_batch - 1

    def x_map(i):
        return ((i // tiles_per_batch) * seq_tiles + jnp.maximum(i % tiles_per_batch - 1, 0), 0)

    return [pl.BlockSpec((TM, D_MODEL), x_map), _resident((N_META, D_MODEL))]


def _inproj_kernel(*refs, tiles_per_batch, first):
    n_hidden = 2 if first else 1
    hidden = refs[:n_hidden]
    (n1_ref, w_ref, cq_ref, csc_ref, alog_ref, dtb_ref,
     qkv_ref, zs_ref, gates_ref, gatest_ref, sgg_ref,
     halo_qkv_ref, halo_sc_ref) = refs[n_hidden:]
    tile_in_batch = pl.program_id(0) % tiles_per_batch
    tm = qkv_ref.shape[0]

    @pl.when(tile_in_batch == 0)
    def _():
        halo_qkv_ref[...] = jnp.zeros_like(halo_qkv_ref)
        halo_sc_ref[...] = jnp.zeros_like(halo_sc_ref)

    x = _input_tile(*hidden, tile_in_batch) if first else hidden[0][...]
    xn = _rms(x, n1_ref[...]).astype(BF16)

    def proj(off, width=D_MODEL):
        return jnp.dot(xn, w_ref[:, off:off + width], preferred_element_type=F32)

    def causal_conv(raw, halo_ref, col, taps_ref, n_taps):
        width = raw.shape[1]
        ext = jnp.concatenate([halo_ref[:, col:col + width], raw], axis=0)
        halo_ref[:, col:col + width] = raw[tm - SUBLANES:tm, :]
        taps = [taps_ref[j:j + 1, col:col + width] for j in range(n_taps)]
        if n_taps == 4:
            delayed = jnp.concatenate([ext[0:1], ext[0:SUBLANES + tm - 1]], axis=0)
            older = ext * taps[1] + delayed * taps[0]
            return (raw * taps[3] + delayed[SUBLANES:] * taps[2]
                    + older[SUBLANES - 2:SUBLANES - 2 + tm])
        acc = raw * taps[n_taps - 1]
        for j in range(n_taps - 1):
            start = SUBLANES - (n_taps - 1) + j
            acc = acc + ext[start:start + tm, :] * taps[j]
        return acc

    def head_l2norm(y, scale):
        outs = []
        for hd in range(y.shape[1] // DN_HEAD_DIM):
            yh = y[:, hd * DN_HEAD_DIM:(hd + 1) * DN_HEAD_DIM]
            inv = lax.rsqrt(jnp.sum(yh * yh, axis=-1, keepdims=True) + EPS)
            outs.append(yh * (inv * scale))
        return jnp.concatenate(outs, axis=-1)

    yq = _silu(causal_conv(proj(OFF_Q), halo_qkv_ref, 0, cq_ref, DN_CONV))
    qkv_ref[:, 0:D_MODEL] = head_l2norm(yq, DN_HEAD_DIM ** -0.5).astype(BF16)
    yk = _silu(causal_conv(proj(OFF_K), halo_qkv_ref, D_MODEL, cq_ref, DN_CONV))
    qkv_ref[:, D_MODEL:2 * D_MODEL] = head_l2norm(yk, 1.0).astype(BF16)
    yv = _silu(causal_conv(proj(OFF_V), halo_qkv_ref, 2 * D_MODEL, cq_ref, DN_CONV))
    qkv_ref[:, 2 * D_MODEL:3 * D_MODEL] = yv.astype(BF16)
    zs_ref[...] = _silu(proj(OFF_Z)).astype(BF16)

    cu = proj(OFF_C) * proj(OFF_U)
    sgg_ref[:, 0:D_MODEL] = (
        proj(OFF_B) * causal_conv(cu, halo_sc_ref, 0, csc_ref, SC_CONV)).astype(BF16)
    sgg_ref[:, D_MODEL:2 * D_MODEL] = jax.nn.sigmoid(proj(OFF_GA)).astype(BF16)
    sgg_ref[:, 2 * D_MODEL:3 * D_MODEL] = jax.nn.sigmoid(proj(OFF_GB)).astype(BF16)

    ba = proj(OFF_BA, LANES)
    lane = lax.broadcasted_iota(jnp.int32, ba.shape, 1)
    g = -jnp.exp(alog_ref[...]) * jax.nn.softplus(ba + dtb_ref[...])
    gates = jnp.where(lane < DN_HEADS, jax.nn.sigmoid(ba), jnp.where(lane < 2 * DN_HEADS, g, 0.0))
    gates_ref[...] = gates
    gates_t = gates.T
    for c in range(tm // CHUNK):
        gatest_ref[c] = gates_t[0:2 * DN_HEADS, c * CHUNK:(c + 1) * CHUNK]


def _inproj(hidden, layer, n1, w_packed, cq, csc, alog_rows, dtb_rows, rows, tiles_per_batch):
    first = len(hidden) == 2

    def rows_of(width):
        return pl.BlockSpec((TM, width), lambda i: (i, 0))

    return pl.pallas_call(
        functools.partial(_inproj_kernel, tiles_per_batch=tiles_per_batch, first=first),
        grid=(rows // TM,),
        in_specs=_hidden_specs(first, tiles_per_batch) + [
            _layer_block(layer, (1, D_MODEL)),
            _layer_block(layer, (D_MODEL, PACKED_WIDTH)),
            _layer_block(layer, (DN_CONV, 3 * D_MODEL)),
            _layer_block(layer, (SC_CONV, D_MODEL)),
            _layer_block(layer, (1, LANES)),
            _layer_block(layer, (1, LANES)),
        ],
        out_specs=[
            rows_of(3 * D_MODEL), rows_of(D_MODEL), rows_of(LANES),
            pl.BlockSpec((TM // CHUNK, 2 * DN_HEADS, CHUNK), lambda i: (i, 0, 0)),
            rows_of(3 * D_MODEL),
        ],
        out_shape=[
            jax.ShapeDtypeStruct((rows, 3 * D_MODEL), BF16),
            jax.ShapeDtypeStruct((rows, D_MODEL), BF16),
            jax.ShapeDtypeStruct((rows, LANES), F32),
            jax.ShapeDtypeStruct((rows // CHUNK, 2 * DN_HEADS, CHUNK), F32),
            jax.ShapeDtypeStruct((rows, 3 * D_MODEL), BF16),
        ],
        scratch_shapes=[
            pltpu.VMEM((SUBLANES, 3 * D_MODEL), F32),
            pltpu.VMEM((SUBLANES, D_MODEL), F32),
        ],
        compiler_params=pltpu.CompilerParams(
            dimension_semantics=("arbitrary",), vmem_limit_bytes=VMEM_LIMIT),
        name="inproj",
    )(*hidden, n1, w_packed, cq, csc, alog_rows, dtb_rows)


def _split3(x):
    x1 = x.astype(BF16)
    r1 = x - x1.astype(F32)
    x2 = r1.astype(BF16)
    x3 = (r1 - x2.astype(F32)).astype(BF16)
    return x1, x2, x3


def _delta_kernel(q_ref, k_ref, v_ref, zs_ref, gates_ref, gatest_ref, onorm_ref, og_ref,
                  state_ref, lhs1_ref, lhs2_ref, u_ref, gl_ref):
    @pl.when(pl.program_id(0) == 0)
    def _():
        state_ref[...] = jnp.zeros_like(state_ref)

    bsz, tc, _ = q_ref.shape
    n_chunks = tc // CHUNK
    n_slots = bsz * n_chunks
    row = lax.broadcasted_iota(jnp.int32, (CHUNK, 2 * CHUNK), 0)
    lane = lax.broadcasted_iota(jnp.int32, (CHUNK, 2 * CHUNK), 1)
    col = lane & (CHUNK - 1)
    left = lane < CHUNK
    causal = row >= col
    strict = row > col
    same16 = (row // 16) == (col // 16)
    same32 = (row // 32) == (col // 32)
    eye = jnp.where(row == col, 1.0, 0.0)
    tril = jnp.where(causal[:, 0:CHUNK], 1.0, 0.0).astype(BF16)
    triu_l = jnp.where(left & (row <= col), 1.0, 0.0).astype(BF16)
    triu_r = jnp.where(jnp.logical_not(left) & (row <= col), 1.0, 0.0).astype(BF16)
    zeros_bf = jnp.zeros((CHUNK, DN_HEAD_DIM), BF16)
    zeros_f = jnp.zeros((CHUNK, DN_HEAD_DIM), F32)
    onorm = onorm_ref[...]
    problems = [(c, hd) for c in range(n_slots) for hd in range(DN_HEADS)]
    pairs = [(c, j) for c in range(n_slots) for j in range(DN_HEADS // 2)]
    nt_dims = (((1,), (1,)), ((), ()))

    def tokens(slot):
        return slot // n_chunks, slice((slot % n_chunks) * CHUNK, (slot % n_chunks + 1) * CHUNK)

    def head_tile(slot, hd):
        return tokens(slot) + (slice(hd * DN_HEAD_DIM, (hd + 1) * DN_HEAD_DIM),)

    def dot3(a_parts, b):
        return sum(jnp.dot(a, b, preferred_element_type=F32) for a in a_parts)

    def half_pad(x, hd, zeros):
        return jnp.concatenate([x, zeros] if hd % 2 == 0 else [zeros, x], axis=0)

    def blockdiag(y):
        return jnp.concatenate([jnp.where(left, y, 0.0), jnp.where(left, 0.0, y)], axis=0).astype(BF16)

    gates, gcol, grow_l, grow_r = [], [], [], []
    for c in range(n_slots):
        gt = gates_ref[tokens(c) + (slice(None),)]
        gates.append(gt)
        g_parts = _split3(gt)
        gcol.append(sum(jnp.dot(tril, gp, preferred_element_type=F32) for gp in g_parts))
        t_parts = _split3(gatest_ref[c // n_chunks, c % n_chunks])
        grow_l.append(dot3(t_parts, triu_l))
        grow_r.append(dot3(t_parts, triu_r))

    gc_b, kq_lhs, k_bf, rhs = {}, {}, {}, {}
    for p in problems:
        c, hd = p
        q = q_ref[head_tile(c, hd)].astype(F32)
        k_bf[p] = k_ref[head_tile(c, hd)]
        k = k_bf[p].astype(F32)
        v = v_ref[head_tile(c, hd)].astype(F32)
        beta = jnp.broadcast_to(gates[c][:, hd:hd + 1], (CHUNK, DN_HEAD_DIM))
        gc = jnp.broadcast_to(gcol[c][:, DN_HEADS + hd:DN_HEADS + hd + 1], (CHUNK, DN_HEAD_DIM))
        gc_b[p] = gc
        g_last = gc[CHUNK - 1:CHUNK, :]
        e_gc = jnp.exp(gc)
        kb = k * beta
        kdec = k * jnp.exp(g_last - gc)
        kq_lhs[p] = jnp.concatenate([kb, q], axis=0).astype(BF16)
        rhs[p] = half_pad(jnp.concatenate([v * beta, kb * e_gc], axis=1).astype(BF16), hd,
                          jnp.zeros((CHUNK, 2 * DN_HEAD_DIM), BF16))
        lhs1_ref[c, hd, CHUNK:2 * CHUNK, :] = (q * e_gc).astype(BF16)
        lhs2_ref[c, hd, CHUNK:CHUNK + DN_HEAD_DIM, :] = half_pad(kdec, hd, zeros_f).T.astype(BF16)
        gl_ref[c * DN_HEADS + hd:c * DN_HEADS + hd + 1, :] = jnp.exp(g_last)

    def level(fn):
        return {p: fn(p) for p in pairs}

    def heads_of(p):
        return (p[0], 2 * p[1]), (p[0], 2 * p[1] + 1)

    def pair_decay(p):
        h0, h1 = heads_of(p)
        c = p[0]
        grow = (grow_l[c][DN_HEADS + h0[1]:DN_HEADS + h0[1] + 1, :]
                + grow_r[c][DN_HEADS + h1[1]:DN_HEADS + h1[1] + 1, :])
        gc = jnp.where(left, gc_b[h0], gc_b[h1])
        return jnp.exp(jnp.where(causal, gc - grow, -jnp.inf))

    def pair_kq(p):
        h0, h1 = heads_of(p)
        keys = jnp.concatenate([k_bf[h0], k_bf[h1]], axis=0)
        both = lax.dot_general(jnp.concatenate([kq_lhs[h0], kq_lhs[h1]], axis=0), keys, nt_dims,
                               preferred_element_type=F32)
        return jnp.where(jnp.concatenate([left, left], axis=0),
                         both[0:2 * CHUNK], both[2 * CHUNK:4 * CHUNK])

    decay = level(pair_decay)
    kq = level(pair_kq)
    a_mat = level(lambda p: jnp.where(strict, kq[p][0:CHUNK] * decay[p], 0.0))
    for p in pairs:
        qk = (kq[p][CHUNK:2 * CHUNK] * decay[p]).astype(BF16)
        for c, hd in heads_of(p):
            lhs2_ref[c, hd, 0:CHUNK, :] = qk

    def pdot(x, y_bd):
        return jnp.dot(x.astype(BF16), y_bd, preferred_element_type=F32)

    n1 = level(lambda p: jnp.where(same16, -a_mat[p], 0.0))
    n2 = level(lambda p: pdot(n1[p], blockdiag(n1[p])))
    t = level(lambda p: eye + n1[p])
    tn = level(lambda p: pdot(jnp.concatenate([t[p], n2[p]], axis=0), blockdiag(n2[p])))
    t = level(lambda p: t[p] + tn[p][0:CHUNK])
    n4 = level(lambda p: tn[p][CHUNK:2 * CHUNK])
    tn = level(lambda p: pdot(jnp.concatenate([t[p], n4[p]], axis=0), blockdiag(n4[p])))
    t = level(lambda p: t[p] + tn[p][0:CHUNK])
    t = level(lambda p: t[p] + pdot(t[p], blockdiag(tn[p][CHUNK:2 * CHUNK])))
    a1 = level(lambda p: jnp.where(same32 & jnp.logical_not(same16), a_mat[p], 0.0))
    m = level(lambda p: pdot(a1[p], blockdiag(t[p])))
    t = level(lambda p: t[p] - pdot(t[p], blockdiag(m[p])))
    a2 = level(lambda p: jnp.where(same32, 0.0, a_mat[p]))
    m = level(lambda p: pdot(a2[p], blockdiag(t[p])))
    t = level(lambda p: (t[p] - pdot(t[p], blockdiag(m[p]))).astype(BF16))
    for p in pairs:
        for c, hd in heads_of(p):
            uw = jnp.dot(t[p], rhs[(c, hd)], preferred_element_type=F32)
            u_ref[c, hd] = uw[:, 0:DN_HEAD_DIM]
            lhs1_ref[c, hd, 0:CHUNK, :] = uw[:, DN_HEAD_DIM:2 * DN_HEAD_DIM].astype(BF16)

    chains = [(b, hd) for b in range(bsz) for hd in range(DN_HEADS)]
    state = {ch: state_ref[ch[0] * DN_HEADS + ch[1]] for ch in chains}
    for c in range(n_chunks):
        r1, r2 = {}, {}
        for b, hd in chains:
            r1[b, hd] = jnp.dot(lhs1_ref[b * n_chunks + c, hd], state[b, hd].astype(BF16),
                                preferred_element_type=F32)
        for b, hd in chains:
            slot = b * n_chunks + c
            v_new = half_pad((u_ref[slot, hd] - r1[b, hd][0:CHUNK]).astype(BF16), hd, zeros_bf)
            r2[b, hd] = jnp.dot(lhs2_ref[slot, hd], v_new, preferred_element_type=F32)
        for b, hd in chains:
            slot = b * n_chunks + c
            gl = gl_ref[slot * DN_HEADS + hd:slot * DN_HEADS + hd + 1, :]
            state[b, hd] = state[b, hd] * gl + r2[b, hd][CHUNK:CHUNK + DN_HEAD_DIM]
            o = r1[b, hd][CHUNK:2 * CHUNK] + r2[b, hd][0:CHUNK]
            og_ref[head_tile(slot, hd)] = (
                _rms(o, onorm) * zs_ref[head_tile(slot, hd)].astype(F32)).astype(BF16)
    for b, hd in chains:
        state_ref[b * DN_HEADS + hd] = state[b, hd]


def _delta(qkv, zs, gates, gatest, layer, onorm, bsz):
    rows = qkv.shape[0]
    lp = rows // bsz
    n_chunks = TC // CHUNK
    n_slots = bsz * n_chunks

    def per_batch(a):
        return a.reshape((bsz, a.shape[0] // bsz) + a.shape[1:])

    def column_block(j):
        return pl.BlockSpec((bsz, TC, D_MODEL), lambda i: (0, i, j))

    row_spec = column_block(0)
    qkv = per_batch(qkv)
    og = pl.pallas_call(
        _delta_kernel,
        grid=(lp // TC,),
        in_specs=[
            column_block(0), column_block(1), column_block(2), row_spec,
            pl.BlockSpec((bsz, TC, LANES), lambda i: (0, i, 0)),
            pl.BlockSpec((bsz, n_chunks, 2 * DN_HEADS, CHUNK), lambda i: (0, i, 0, 0)),
            _layer_block(layer, (1, DN_HEAD_DIM)),
        ],
        out_specs=row_spec,
        out_shape=jax.ShapeDtypeStruct((bsz, lp, D_MODEL), BF16),
        scratch_shapes=[
            pltpu.VMEM((bsz * DN_HEADS, DN_HEAD_DIM, DN_HEAD_DIM), F32),
            pltpu.VMEM((n_slots, DN_HEADS, 2 * CHUNK, DN_HEAD_DIM), BF16),
            pltpu.VMEM((n_slots, DN_HEADS, CHUNK + DN_HEAD_DIM, 2 * CHUNK), BF16),
            pltpu.VMEM((n_slots, DN_HEADS, CHUNK, DN_HEAD_DIM), F32),
            pltpu.VMEM((n_slots * DN_HEADS, DN_HEAD_DIM), F32),
        ],
        compiler_params=pltpu.CompilerParams(
            dimension_semantics=("arbitrary",), vmem_limit_bytes=VMEM_LIMIT),
        name="delta_rule",
    )(qkv, qkv, qkv, *(per_batch(a) for a in (zs, gates, gatest)), onorm)
    return og.reshape(rows, D_MODEL)


FF_SPLITS = ((0, 1024), (1024, 2048), (2048, D_FF))


def _ffn_kernel(*refs, tiles_per_batch, first, final):
    n_hidden = 2 if first else 1
    hidden = refs[:n_hidden]
    (og_ref, s_ref, ga_ref, gb_ref, wdn_ref, wsc_ref, wo_ref, n2_ref, wgu_ref, wdown_ref,
     fnorm_ref, out_ref) = refs[n_hidden:]
    tile_in_batch = pl.program_id(0) % tiles_per_batch

    def body():
        h = _input_tile(*hidden, tile_in_batch) if first else hidden[0][...]
        ya = jnp.dot(og_ref[...], wdn_ref[...], preferred_element_type=F32)
        yb = jnp.dot(s_ref[...], wsc_ref[...], preferred_element_type=F32)
        mixed = ga_ref[...].astype(F32) * ya + gb_ref[...].astype(F32) * yb
        h1 = h + _bdot(mixed, wo_ref[...])
        hn = _rms(h1, n2_ref[...]).astype(BF16)
        acc = h1
        for lo, hi in FF_SPLITS:
            gate = jnp.dot(hn, wgu_ref[:, lo:hi], preferred_element_type=F32)
            up = jnp.dot(hn, wgu_ref[:, D_FF + lo:D_FF + hi], preferred_element_type=F32)
            acc = acc + _bdot(_silu(gate) * up, wdown_ref[lo:hi, :])
        out_ref[...] = _rms(acc, fnorm_ref[...]) if final else acc

    if final:
        pl.when(tile_in_batch != 0)(body)
    else:
        body()


def _ffn(hidden, og, sgg, layer, wdn, wsc, wo, n2, wgu, wdown, fnorm, rows, tiles_per_batch,
         final):
    first = len(hidden) == 2
    row_spec = pl.BlockSpec((TM, D_MODEL), lambda i: (i, 0))

    def sgg_block(j):
        return pl.BlockSpec((TM, D_MODEL), lambda i: (i, j))

    if final:
        seq_tiles = tiles_per_batch - 1
        out_rows = rows // tiles_per_batch * seq_tiles
        out_spec = pl.BlockSpec(
            (TM, D_MODEL),
            lambda i: ((i // tiles_per_batch) * seq_tiles + jnp.maximum(i % tiles_per_batch - 1, 0), 0))
    else:
        out_rows, out_spec = rows, row_spec
    return pl.pallas_call(
        functools.partial(_ffn_kernel, tiles_per_batch=tiles_per_batch, first=first, final=final),
        grid=(rows // TM,),
        in_specs=_hidden_specs(first, tiles_per_batch) + [
            row_spec, sgg_block(0), sgg_block(1), sgg_block(2),
            _layer_block(layer, (D_MODEL, D_MODEL)),
            _layer_block(layer, (D_MODEL, D_MODEL)),
            _layer_block(layer, (D_MODEL, D_MODEL)),
            _layer_block(layer, (1, D_MODEL)),
            _layer_block(layer, (D_MODEL, 2 * D_FF)),
            _layer_block(layer, (D_FF, D_MODEL)),
            _resident((1, D_MODEL)),
        ],
        out_specs=out_spec,
        out_shape=jax.ShapeDtypeStruct((out_rows, D_MODEL), F32),
        compiler_params=pltpu.CompilerParams(
            dimension_semantics=("arbitrary",), vmem_limit_bytes=VMEM_LIMIT),
        name="mix_ffn",
    )(*hidden, og, sgg, sgg, sgg, wdn, wsc, wo, n2, wgu, wdown, fnorm)


def _pack_kernel(wt_ref, out_ref):
    g0, g1 = W_IN_GATES.start, W_IN_GATES.stop
    out_ref[:, 0:g0] = wt_ref[0:g0, :].T.astype(BF16)
    gates = wt_ref[g0:g0 + LANES, :].T
    lane = lax.broadcasted_iota(jnp.int32, gates.shape, 1)
    out_ref[:, g0:g0 + LANES] = jnp.where(lane < g1 - g0, gates, 0.0).astype(BF16)
    out_ref[:, g0 + LANES:] = wt_ref[g1:, :].T.astype(BF16)


def _pack_w_in(w_in):
    depth, d, width = w_in.shape
    return pl.pallas_call(
        _pack_kernel,
        grid=(depth, d // LANES),
        in_specs=[pl.BlockSpec((None, width, LANES), lambda l, i: (l, 0, i))],
        out_specs=pl.BlockSpec((None, LANES, PACKED_WIDTH), lambda l, i: (l, i, 0)),
        out_shape=jax.ShapeDtypeStruct((depth, d, PACKED_WIDTH), BF16),
        compiler_params=pltpu.CompilerParams(
            dimension_semantics=("arbitrary", "arbitrary"), vmem_limit_bytes=VMEM_LIMIT),
        name="pack_w_in",
    )(jnp.swapaxes(w_in, 1, 2))


def _gate_rows(p):
    return jnp.pad(p.astype(F32), ((0, 0), (DN_HEADS, LANES - 2 * DN_HEADS)))[:, None, :]


def kernel(x, meta_tokens, norm1, w_in, conv_qkv, a_log, dt_bias, o_norm, w_dn_out, conv_sc,
           w_sc_out, w_o, norm2, w_gate_up, w_down, final_norm):
    bsz, seq, d = x.shape
    depth = w_in.shape[0]
    assert d == D_MODEL and seq % TM == 0 and TM % TC == 0 and TM >= N_META
    tiles_per_batch = seq // TM + 1
    rows = bsz * tiles_per_batch * TM

    w_packed = _pack_w_in(w_in)
    wdn, wsc, wo = w_dn_out.astype(BF16), w_sc_out.astype(BF16), w_o.astype(BF16)
    wgu, wdown = w_gate_up.astype(BF16), w_down.astype(BF16)
    n1, n2, onorm = norm1[:, None, :], norm2[:, None, :], o_norm[:, None, :]
    alog_rows, dtb_rows = _gate_rows(a_log), _gate_rows(dt_bias)
    fnorm = final_norm.reshape(1, d)

    hidden = (x.reshape(bsz * seq, d), meta_tokens.astype(x.dtype))
    for layer in range(depth):
        qkv, zs, gates, gatest, sgg = _inproj(
            hidden, layer, n1, w_packed, conv_qkv, conv_sc, alog_rows, dtb_rows,
            rows, tiles_per_batch)
        og = _delta(qkv, zs, gates, gatest, layer, onorm, bsz)
        h = _ffn(hidden, og, sgg, layer, wdn, wsc, wo, n2, wgu, wdown, fnorm,
                 rows, tiles_per_batch, layer == depth - 1)
        hidden = (h,)
    return hidden[0].reshape(bsz, seq, d)
```

```python
import functools

import jax
import jax.numpy as jnp
from jax import lax
from jax.experimental import pallas as pl
from jax.experimental.pallas import tpu as pltpu

D_MODEL = 1024
N_META = 16
DN_HEADS = 8
DN_HEAD_DIM = 128
DN_CONV = 4
SC_CONV = 3
CHUNK = 64
D_FF = 2816
EPS = 1e-6

LANES = 128
SUBLANES = 8
TM = 256
TC = 256
VMEM_LIMIT = 56 * 2 ** 20
INV_BLOCK = 16

W_IN_GATES = slice(4 * D_MODEL, 4 * D_MODEL + 2 * DN_HEADS)
OFF_Q, OFF_K, OFF_V, OFF_Z = (j * D_MODEL for j in range(4))
OFF_BA = 4 * D_MODEL
OFF_C, OFF_U, OFF_B, OFF_GA, OFF_GB = (OFF_BA + LANES + j * D_MODEL for j in range(5))
PACKED_WIDTH = OFF_GB + D_MODEL

F32 = jnp.float32
BF16 = jnp.bfloat16


def _bdot(a, b):
    return jnp.dot(a.astype(BF16), b.astype(BF16), preferred_element_type=F32)


def _silu(x):
    return x * jax.nn.sigmoid(x)


def _rms(x, gain):
    return x * lax.rsqrt(jnp.mean(x * x, axis=-1, keepdims=True) + EPS) * gain


def _layer_block(layer, shape):
    return pl.BlockSpec((None,) + shape, lambda *_: (layer,) + (0,) * len(shape),
                        pipeline_mode=pl.Buffered(1))


def _resident(shape):
    return pl.BlockSpec(shape, lambda *_: (0,) * len(shape), pipeline_mode=pl.Buffered(1))


def _input_tile(x_ref, meta_ref, tile_in_batch):
    tm, d = x_ref.shape
    front = jnp.concatenate([jnp.zeros((tm - N_META, d), F32), meta_ref[...]], axis=0)
    return jnp.where(tile_in_batch == 0, front, x_ref[...])


def _hidden_specs(first, tiles_per_batch):
    if not first:
        return [pl.BlockSpec((TM, D_MODEL), lambda i: (i, 0))]
    seq_tiles = tiles_per_batch - 1

    def x_map(i):
        return ((i // tiles_per_batch) * seq_tiles + jnp.maximum(i % tiles_per_batch - 1, 0), 0)

    return [pl.BlockSpec((TM, D_MODEL), x_map), _resident((N_META, D_MODEL))]


def _inproj_kernel(*refs, tiles_per_batch, first):
    n_hidden = 2 if first else 1
    hidden = refs[:n_hidden]
    (n1_ref, w_ref, cq_ref, csc_ref, alog_ref, dtb_ref,
     qkv_ref, zs_ref, gates_ref, gatest_ref, sgg_ref,
     halo_qkv_ref, halo_sc_ref) = refs[n_hidden:]
    tile_in_batch = pl.program_id(0) % tiles_per_batch
    tm = qkv_ref.shape[0]

    @pl.when(tile_in_batch == 0)
    def _():
        halo_qkv_ref[...] = jnp.zeros_like(halo_qkv_ref)
        halo_sc_ref[...] = jnp.zeros_like(halo_sc_ref)

    x = _input_tile(*hidden, tile_in_batch) if first else hidden[0][...]
    xn = _rms(x, n1_ref[...]).astype(BF16)

    def proj(off, width=D_MODEL):
        return jnp.dot(xn, w_ref[:, off:off + width], preferred_element_type=F32)

    def causal_conv(raw, halo_ref, col, taps_ref, n_taps):
        width = raw.shape[1]
        ext = jnp.concatenate([halo_ref[:, col:col + width], raw], axis=0)
        halo_ref[:, col:col + width] = raw[tm - SUBLANES:tm, :]
        taps = [taps_ref[j:j + 1, col:col + width] for j in range(n_taps)]
        if n_taps == 4:
            delayed = jnp.concatenate([ext[0:1], ext[0:SUBLANES + tm - 1]], axis=0)
            older = ext * taps[1] + delayed * taps[0]
            return (raw * taps[3] + delayed[SUBLANES:] * taps[2]
                    + older[SUBLANES - 2:SUBLANES - 2 + tm])
        acc = raw * taps[n_taps - 1]
        for j in range(n_taps - 1):
            start = SUBLANES - (n_taps - 1) + j
            acc = acc + ext[start:start + tm, :] * taps[j]
        return acc

    def head_l2norm(y, scale):
        outs = []
        for hd in range(y.shape[1] // DN_HEAD_DIM):
            yh = y[:, hd * DN_HEAD_DIM:(hd + 1) * DN_HEAD_DIM]
            inv = lax.rsqrt(jnp.sum(yh * yh, axis=-1, keepdims=True) + EPS)
            outs.append(yh * (inv * scale))
        return jnp.concatenate(outs, axis=-1)

    yq = _silu(causal_conv(proj(OFF_Q), halo_qkv_ref, 0, cq_ref, DN_CONV))
    qkv_ref[:, 0:D_MODEL] = head_l2norm(yq, DN_HEAD_DIM ** -0.5).astype(BF16)
    yk = _silu(causal_conv(proj(OFF_K), halo_qkv_ref, D_MODEL, cq_ref, DN_CONV))
    qkv_ref[:, D_MODEL:2 * D_MODEL] = head_l2norm(yk, 1.0).astype(BF16)
    yv = _silu(causal_conv(proj(OFF_V), halo_qkv_ref, 2 * D_MODEL, cq_ref, DN_CONV))
    qkv_ref[:, 2 * D_MODEL:3 * D_MODEL] = yv.astype(BF16)
    zs_ref[...] = _silu(proj(OFF_Z)).astype(BF16)

    cu = proj(OFF_C) * proj(OFF_U)
    sgg_ref[:, 0:D_MODEL] = (
        proj(OFF_B) * causal_conv(cu, halo_sc_ref, 0, csc_ref, SC_CONV)).astype(BF16)
    sgg_ref[:, D_MODEL:2 * D_MODEL] = jax.nn.sigmoid(proj(OFF_GA)).astype(BF16)
    sgg_ref[:, 2 * D_MODEL:3 * D_MODEL] = jax.nn.sigmoid(proj(OFF_GB)).astype(BF16)

    ba = proj(OFF_BA, LANES)
    lane = lax.broadcasted_iota(jnp.int32, ba.shape, 1)
    g = -jnp.exp(alog_ref[...]) * jax.nn.softplus(ba + dtb_ref[...])
    gates = jnp.where(lane < DN_HEADS, jax.nn.sigmoid(ba), jnp.where(lane < 2 * DN_HEADS, g, 0.0))
    gates_ref[...] = gates
    gates_t = gates.T
    for c in range(tm // CHUNK):
        gatest_ref[c] = gates_t[0:2 * DN_HEADS, c * CHUNK:(c + 1) * CHUNK]


def _inproj(hidden, layer, n1, w_packed, cq, csc, alog_rows, dtb_rows, rows, tiles_per_batch):
    first = len(hidden) == 2

    def rows_of(width):
        return pl.BlockSpec((TM, width), lambda i: (i, 0))

    return pl.pallas_call(
        functools.partial(_inproj_kernel, tiles_per_batch=tiles_per_batch, first=first),
        grid=(rows // TM,),
        in_specs=_hidden_specs(first, tiles_per_batch) + [
            _layer_block(layer, (1, D_MODEL)),
            _layer_block(layer, (D_MODEL, PACKED_WIDTH)),
            _layer_block(layer, (DN_CONV, 3 * D_MODEL)),
            _layer_block(layer, (SC_CONV, D_MODEL)),
            _layer_block(layer, (1, LANES)),
            _layer_block(layer, (1, LANES)),
        ],
        out_specs=[
            rows_of(3 * D_MODEL), rows_of(D_MODEL), rows_of(LANES),
            pl.BlockSpec((TM // CHUNK, 2 * DN_HEADS, CHUNK), lambda i: (i, 0, 0)),
            rows_of(3 * D_MODEL),
        ],
        out_shape=[
            jax.ShapeDtypeStruct((rows, 3 * D_MODEL), BF16),
            jax.ShapeDtypeStruct((rows, D_MODEL), BF16),
            jax.ShapeDtypeStruct((rows, LANES), F32),
            jax.ShapeDtypeStruct((rows // CHUNK, 2 * DN_HEADS, CHUNK), F32),
            jax.ShapeDtypeStruct((rows, 3 * D_MODEL), BF16),
        ],
        scratch_shapes=[
            pltpu.VMEM((SUBLANES, 3 * D_MODEL), F32),
            pltpu.VMEM((SUBLANES, D_MODEL), F32),
        ],
        compiler_params=pltpu.CompilerParams(
            dimension_semantics=("arbitrary",), vmem_limit_bytes=VMEM_LIMIT),
        name="inproj",
    )(*hidden, n1, w_packed, cq, csc, alog_rows, dtb_rows)


def _split3(x):
    x1 = x.astype(BF16)
    r1 = x - x1.astype(F32)
    x2 = r1.astype(BF16)
    x3 = (r1 - x2.astype(F32)).astype(BF16)
    return x1, x2, x3


def _delta_kernel(q_ref, k_ref, v_ref, zs_ref, gates_ref, gatest_ref, onorm_ref, og_ref,
                  state_ref, lhs1_ref, lhs2_ref, u_ref, gl_ref):
    @pl.when(pl.program_id(0) == 0)
    def _():
        state_ref[...] = jnp.zeros_like(state_ref)

    bsz, tc, _ = q_ref.shape
    n_chunks = tc // CHUNK
    n_slots = bsz * n_chunks
    row = lax.broadcasted_iota(jnp.int32, (CHUNK, 2 * CHUNK), 0)
    lane = lax.broadcasted_iota(jnp.int32, (CHUNK, 2 * CHUNK), 1)
    col = lane & (CHUNK - 1)
    left = lane < CHUNK
    causal = row >= col
    strict = row > col
    same16 = (row // INV_BLOCK) == (col // INV_BLOCK)
    eye = jnp.where(row == col, 1.0, 0.0)
    tril = jnp.where(causal[:, 0:CHUNK], 1.0, 0.0).astype(BF16)
    triu_l = jnp.where(left & (row <= col), 1.0, 0.0).astype(BF16)
    triu_r = jnp.where(jnp.logical_not(left) & (row <= col), 1.0, 0.0).astype(BF16)
    zeros_bf = jnp.zeros((CHUNK, DN_HEAD_DIM), BF16)
    zeros_f = jnp.zeros((CHUNK, DN_HEAD_DIM), F32)
    onorm = onorm_ref[...]
    problems = [(c, hd) for c in range(n_slots) for hd in range(DN_HEADS)]
    pairs = [(c, j) for c in range(n_slots) for j in range(DN_HEADS // 2)]
    nt_dims = (((1,), (1,)), ((), ()))

    def tokens(slot):
        return slot // n_chunks, slice((slot % n_chunks) * CHUNK, (slot % n_chunks + 1) * CHUNK)

    def head_tile(slot, hd):
        return tokens(slot) + (slice(hd * DN_HEAD_DIM, (hd + 1) * DN_HEAD_DIM),)

    def dot3(a_parts, b):
        return sum(jnp.dot(a, b, preferred_element_type=F32) for a in a_parts)

    def half_pad(x, hd, zeros):
        return jnp.concatenate([x, zeros] if hd % 2 == 0 else [zeros, x], axis=0)

    def blockdiag(y):
        return jnp.concatenate([jnp.where(left, y, 0.0), jnp.where(left, 0.0, y)], axis=0).astype(BF16)

    gates, gcol, grow_l, grow_r = [], [], [], []
    for c in range(n_slots):
        gt = gates_ref[tokens(c) + (slice(None),)]
        gates.append(gt)
        g_parts = _split3(gt)
        gcol.append(sum(jnp.dot(tril, gp, preferred_element_type=F32) for gp in g_parts))
        t_parts = _split3(gatest_ref[c // n_chunks, c % n_chunks])
        grow_l.append(dot3(t_parts, triu_l))
        grow_r.append(dot3(t_parts, triu_r))

    gc_b, kq_lhs, k_bf, rhs = {}, {}, {}, {}
    for p in problems:
        c, hd = p
        q = q_ref[head_tile(c, hd)].astype(F32)
        k_bf[p] = k_ref[head_tile(c, hd)]
        k = k_bf[p].astype(F32)
        v = v_ref[head_tile(c, hd)].astype(F32)
        beta = jnp.broadcast_to(gates[c][:, hd:hd + 1], (CHUNK, DN_HEAD_DIM))
        gc = jnp.broadcast_to(gcol[c][:, DN_HEADS + hd:DN_HEADS + hd + 1], (CHUNK, DN_HEAD_DIM))
        gc_b[p] = gc
        g_last = gc[CHUNK - 1:CHUNK, :]
        e_gc = jnp.exp(gc)
        kb = k * beta
        kdec = k * jnp.exp(g_last - gc)
        kq_lhs[p] = jnp.concatenate([kb, q], axis=0).astype(BF16)
        rhs[p] = half_pad(jnp.concatenate([v * beta, kb * e_gc], axis=1).astype(BF16), hd,
                          jnp.zeros((CHUNK, 2 * DN_HEAD_DIM), BF16))
        lhs1_ref[c, hd, CHUNK:2 * CHUNK, :] = (q * e_gc).astype(BF16)
        lhs2_ref[c, hd, CHUNK:CHUNK + DN_HEAD_DIM, :] = half_pad(kdec, hd, zeros_f).T.astype(BF16)
        gl_ref[c * DN_HEADS + hd:c * DN_HEADS + hd + 1, :] = jnp.exp(g_last)

    def level(fn):
        return {p: fn(p) for p in pairs}

    def heads_of(p):
        return (p[0], 2 * p[1]), (p[0], 2 * p[1] + 1)

    def pair_decay(p):
        h0, h1 = heads_of(p)
        c = p[0]
        grow = (grow_l[c][DN_HEADS + h0[1]:DN_HEADS + h0[1] + 1, :]
                + grow_r[c][DN_HEADS + h1[1]:DN_HEADS + h1[1] + 1, :])
        gc = jnp.where(left, gc_b[h0], gc_b[h1])
        return jnp.exp(jnp.where(causal, gc - grow, -jnp.inf))

    def pair_kq(p):
        h0, h1 = heads_of(p)
        keys = jnp.concatenate([k_bf[h0], k_bf[h1]], axis=0)
        both = lax.dot_general(jnp.concatenate([kq_lhs[h0], kq_lhs[h1]], axis=0), keys, nt_dims,
                               preferred_element_type=F32)
        return jnp.where(jnp.concatenate([left, left], axis=0),
                         both[0:2 * CHUNK], both[2 * CHUNK:4 * CHUNK])

    decay = level(pair_decay)
    kq = level(pair_kq)
    a_mat = level(lambda p: jnp.where(strict, kq[p][0:CHUNK] * decay[p], 0.0))
    for p in pairs:
        qk = (kq[p][CHUNK:2 * CHUNK] * decay[p]).astype(BF16)
        for c, hd in heads_of(p):
            lhs2_ref[c, hd, 0:CHUNK, :] = qk

    def pdot(x, y_bd):
        return jnp.dot(x.astype(BF16), y_bd, preferred_element_type=F32)

    n1 = level(lambda p: jnp.where(same16, -a_mat[p], 0.0))
    n2 = level(lambda p: pdot(n1[p], blockdiag(n1[p])))
    t = level(lambda p: eye + n1[p])
    tn = level(lambda p: pdot(jnp.concatenate([t[p], n2[p]], axis=0), blockdiag(n2[p])))
    t = level(lambda p: t[p] + tn[p][0:CHUNK])
    n4 = level(lambda p: tn[p][CHUNK:2 * CHUNK])
    tn = level(lambda p: pdot(jnp.concatenate([t[p], n4[p]], axis=0), blockdiag(n4[p])))
    t = level(lambda p: t[p] + tn[p][0:CHUNK])
    t = level(lambda p: t[p] + pdot(t[p], blockdiag(tn[p][CHUNK:2 * CHUNK])))
    off = level(lambda p: jnp.where(same16, 0.0, a_mat[p]))
    m = level(lambda p: -pdot(off[p], blockdiag(t[p])))
    tm_ = level(lambda p: pdot(jnp.concatenate([t[p], m[p]], axis=0), blockdiag(m[p])))
    t = level(lambda p: t[p] + tm_[p][0:CHUNK])
    t = level(lambda p: (t[p] + pdot(t[p], blockdiag(tm_[p][CHUNK:2 * CHUNK]))).astype(BF16))
    for p in pairs:
        for c, hd in heads_of(p):
            uw = jnp.dot(t[p], rhs[(c, hd)], preferred_element_type=F32)
            u_ref[c, hd] = uw[:, 0:DN_HEAD_DIM]
            lhs1_ref[c, hd, 0:CHUNK, :] = uw[:, DN_HEAD_DIM:2 * DN_HEAD_DIM].astype(BF16)

    chains = [(b, hd) for b in range(bsz) for hd in range(DN_HEADS)]
    state = {ch: state_ref[ch[0] * DN_HEADS + ch[1]] for ch in chains}
    for c in range(n_chunks):
        r1, r2 = {}, {}
        for b, hd in chains:
            r1[b, hd] = jnp.dot(lhs1_ref[b * n_chunks + c, hd], state[b, hd].astype(BF16),
                                preferred_element_type=F32)
        for b, hd in chains:
            slot = b * n_chunks + c
            v_new = half_pad((u_ref[slot, hd] - r1[b, hd][0:CHUNK]).astype(BF16), hd, zeros_bf)
            r2[b, hd] = jnp.dot(lhs2_ref[slot, hd], v_new, preferred_element_type=F32)
        for b, hd in chains:
            slot = b * n_chunks + c
            gl = gl_ref[slot * DN_HEADS + hd:slot * DN_HEADS + hd + 1, :]
            state[b, hd] = state[b, hd] * gl + r2[b, hd][CHUNK:CHUNK + DN_HEAD_DIM]
            o = r1[b, hd][CHUNK:2 * CHUNK] + r2[b, hd][0:CHUNK]
            og_ref[head_tile(slot, hd)] = (
                _rms(o, onorm) * zs_ref[head_tile(slot, hd)].astype(F32)).astype(BF16)
    for b, hd in chains:
        state_ref[b * DN_HEADS + hd] = state[b, hd]


def _delta(qkv, zs, gates, gatest, layer, onorm, bsz):
    rows = qkv.shape[0]
    lp = rows // bsz
    n_chunks = TC // CHUNK
    n_slots = bsz * n_chunks

    def per_batch(a):
        return a.reshape((bsz, a.shape[0] // bsz) + a.shape[1:])

    def column_block(j):
        return pl.BlockSpec((bsz, TC, D_MODEL), lambda i: (0, i, j))

    row_spec = column_block(0)
    qkv = per_batch(qkv)
    og = pl.pallas_call(
        _delta_kernel,
        grid=(lp // TC,),
        in_specs=[
            column_block(0), column_block(1), column_block(2), row_spec,
            pl.BlockSpec((bsz, TC, LANES), lambda i: (0, i, 0)),
            pl.BlockSpec((bsz, n_chunks, 2 * DN_HEADS, CHUNK), lambda i: (0, i, 0, 0)),
            _layer_block(layer, (1, DN_HEAD_DIM)),
        ],
        out_specs=row_spec,
        out_shape=jax.ShapeDtypeStruct((bsz, lp, D_MODEL), BF16),
        scratch_shapes=[
            pltpu.VMEM((bsz * DN_HEADS, DN_HEAD_DIM, DN_HEAD_DIM), F32),
            pltpu.VMEM((n_slots, DN_HEADS, 2 * CHUNK, DN_HEAD_DIM), BF16),
            pltpu.VMEM((n_slots, DN_HEADS, CHUNK + DN_HEAD_DIM, 2 * CHUNK), BF16),
            pltpu.VMEM((n_slots, DN_HEADS, CHUNK, DN_HEAD_DIM), F32),
            pltpu.VMEM((n_slots * DN_HEADS, DN_HEAD_DIM), F32),
        ],
        compiler_params=pltpu.CompilerParams(
            dimension_semantics=("arbitrary",), vmem_limit_bytes=VMEM_LIMIT),
        name="delta_rule",
    )(qkv, qkv, qkv, *(per_batch(a) for a in (zs, gates, gatest)), onorm)
    return og.reshape(rows, D_MODEL)


FF_SPLITS = ((0, D_MODEL), (D_MODEL, 2 * D_MODEL), (2 * D_MODEL, D_FF))


def _ffn_kernel(*refs, tiles_per_batch, first, final):
    n_hidden = 2 if first else 1
    hidden = refs[:n_hidden]
    (og_ref, s_ref, ga_ref, gb_ref, wdn_ref, wsc_ref, wo_ref, n2_ref, wgu_ref, wdown_ref,
     fnorm_ref, out_ref) = refs[n_hidden:]
    tile_in_batch = pl.program_id(0) % tiles_per_batch

    def body():
        h = _input_tile(*hidden, tile_in_batch) if first else hidden[0][...]
        ya = jnp.dot(og_ref[...], wdn_ref[...].astype(BF16), preferred_element_type=F32)
        yb = jnp.dot(s_ref[...], wsc_ref[...].astype(BF16), preferred_element_type=F32)
        mixed = ga_ref[...].astype(F32) * ya + gb_ref[...].astype(F32) * yb
        h1 = h + _bdot(mixed, wo_ref[...])
        hn = _rms(h1, n2_ref[...]).astype(BF16)
        acc = h1
        for lo, hi in FF_SPLITS:
            gate = jnp.dot(hn, wgu_ref[:, lo:hi], preferred_element_type=F32)
            up = jnp.dot(hn, wgu_ref[:, D_FF + lo:D_FF + hi], preferred_element_type=F32)
            acc = acc + _bdot(_silu(gate) * up, wdown_ref[lo:hi, :])
        out_ref[...] = _rms(acc, fnorm_ref[...]) if final else acc

    if final:
        pl.when(tile_in_batch != 0)(body)
    else:
        body()


def _ffn(hidden, og, sgg, layer, wdn, wsc, wo, n2, wgu, wdown, fnorm, rows, tiles_per_batch,
         final):
    first = len(hidden) == 2
    row_spec = pl.BlockSpec((TM, D_MODEL), lambda i: (i, 0))

    def sgg_block(j):
        return pl.BlockSpec((TM, D_MODEL), lambda i: (i, j))

    if final:
        seq_tiles = tiles_per_batch - 1
        out_rows = rows // tiles_per_batch * seq_tiles
        out_spec = pl.BlockSpec(
            (TM, D_MODEL),
            lambda i: ((i // tiles_per_batch) * seq_tiles + jnp.maximum(i % tiles_per_batch - 1, 0), 0))
    else:
        out_rows, out_spec = rows, row_spec
    return pl.pallas_call(
        functools.partial(_ffn_kernel, tiles_per_batch=tiles_per_batch, first=first, final=final),
        grid=(rows // TM,),
        in_specs=_hidden_specs(first, tiles_per_batch) + [
            row_spec, sgg_block(0), sgg_block(1), sgg_block(2),
            _layer_block(layer, (D_MODEL, D_MODEL)),
            _layer_block(layer, (D_MODEL, D_MODEL)),
            _layer_block(layer, (D_MODEL, D_MODEL)),
            _layer_block(layer, (1, D_MODEL)),
            _layer_block(layer, (D_MODEL, 2 * D_FF)),
            _layer_block(layer, (D_FF, D_MODEL)),
            _resident((1, D_MODEL)),
        ],
        out_specs=out_spec,
        out_shape=jax.ShapeDtypeStruct((out_rows, D_MODEL), F32),
        compiler_params=pltpu.CompilerParams(
            dimension_semantics=("arbitrary",), vmem_limit_bytes=VMEM_LIMIT),
        name="mix_ffn",
    )(*hidden, og, sgg, sgg, sgg, wdn, wsc, wo, n2, wgu, wdown, fnorm)


def _pack_kernel(wt_ref, out_ref):
    g0, g1 = W_IN_GATES.start, W_IN_GATES.stop
    out_ref[:, 0:g0] = wt_ref[0:g0, :].T.astype(BF16)
    gates = wt_ref[g0:g0 + LANES, :].T
    lane = lax.broadcasted_iota(jnp.int32, gates.shape, 1)
    out_ref[:, g0:g0 + LANES] = jnp.where(lane < g1 - g0, gates, 0.0).astype(BF16)
    out_ref[:, g0 + LANES:] = wt_ref[g1:, :].T.astype(BF16)


def _pack_w_in(w_in):
    depth, d, width = w_in.shape
    return pl.pallas_call(
        _pack_kernel,
        grid=(depth, d // LANES),
        in_specs=[pl.BlockSpec((None, width, LANES), lambda l, i: (l, 0, i))],
        out_specs=pl.BlockSpec((None, LANES, PACKED_WIDTH), lambda l, i: (l, i, 0)),
        out_shape=jax.ShapeDtypeStruct((depth, d, PACKED_WIDTH), BF16),
        compiler_params=pltpu.CompilerParams(
            dimension_semantics=("arbitrary", "arbitrary"), vmem_limit_bytes=VMEM_LIMIT),
        name="pack_w_in",
    )(jnp.swapaxes(w_in, 1, 2))


def _gate_rows(p):
    return jnp.pad(p.astype(F32), ((0, 0), (DN_HEADS, LANES - 2 * DN_HEADS)))[:, None, :]


def kernel(x, meta_tokens, norm1, w_in, conv_qkv, a_log, dt_bias, o_norm, w_dn_out, conv_sc,
           w_sc_out, w_o, norm2, w_gate_up, w_down, final_norm):
    bsz, seq, d = x.shape
    depth = w_in.shape[0]
    assert d == D_MODEL and seq % TM == 0 and TM % TC == 0 and TM >= N_META
    tiles_per_batch = seq // TM + 1
    rows = bsz * tiles_per_batch * TM

    w_packed = _pack_w_in(w_in)
    wdn, wsc, wo = w_dn_out, w_sc_out, w_o
    wgu, wdown = w_gate_up.astype(BF16), w_down
    n1, n2, onorm = norm1[:, None, :], norm2[:, None, :], o_norm[:, None, :]
    alog_rows, dtb_rows = _gate_rows(a_log), _gate_rows(dt_bias)
    fnorm = final_norm.reshape(1, d)

    hidden = (x.reshape(bsz * seq, d), meta_tokens.astype(x.dtype))
    for layer in range(depth):
        qkv, zs, gates, gatest, sgg = _inproj(
            hidden, layer, n1, w_packed, conv_qkv, conv_sc, alog_rows, dtb_rows,
            rows, tiles_per_batch)
        og = _delta(qkv, zs, gates, gatest, layer, onorm, bsz)
        h = _ffn(hidden, og, sgg, layer, wdn, wsc, wo, n2, wgu, wdown, fnorm,
                 rows, tiles_per_batch, layer == depth - 1)
        hidden = (h,)
    return hidden[0].reshape(bsz, seq, d)
```

```python
import functools

import jax
import jax.numpy as jnp
from jax import lax
from jax.experimental import pallas as pl
from jax.experimental.pallas import tpu as pltpu

D_MODEL = 1024
N_META = 16
DN_HEADS = 8
DN_HEAD_DIM = 128
DN_CONV = 4
SC_CONV = 3
CHUNK = 64
D_FF = 2816
EPS = 1e-6

LANES = 128
SUBLANES = 8
TM = 256
TC = 256
VMEM_LIMIT = 56 * 2 ** 20
INV_BLOCK = 16

W_IN_GATES = slice(4 * D_MODEL, 4 * D_MODEL + 2 * DN_HEADS)
OFF_Q, OFF_K, OFF_V, OFF_Z = (j * D_MODEL for j in range(4))
OFF_BA = 4 * D_MODEL
OFF_C, OFF_U, OFF_B, OFF_GA, OFF_GB = (OFF_BA + LANES + j * D_MODEL for j in range(5))
PACKED_WIDTH = OFF_GB + D_MODEL

F32 = jnp.float32
BF16 = jnp.bfloat16


def _bdot(a, b):
    return jnp.dot(a.astype(BF16), b.astype(BF16), preferred_element_type=F32)


def _silu(x):
    return x * jax.nn.sigmoid(x)


def _rms(x, gain):
    return x * lax.rsqrt(jnp.mean(x * x, axis=-1, keepdims=True) + EPS) * gain


def _layer_block(layer, shape):
    return pl.BlockSpec((None,) + shape, lambda *_: (layer,) + (0,) * len(shape),
                        pipeline_mode=pl.Buffered(1))


def _resident(shape):
    return pl.BlockSpec(shape, lambda *_: (0,) * len(shape), pipeline_mode=pl.Buffered(1))


def _input_tile(x_ref, meta_ref, tile_in_batch):
    tm, d = x_ref.shape
    front = jnp.concatenate([jnp.zeros((tm - N_META, d), F32), meta_ref[...]], axis=0)
    return jnp.where(tile_in_batch == 0, front, x_ref[...])


def _hidden_specs(first, tiles_per_batch):
    if not first:
        return [pl.BlockSpec((TM, D_MODEL), lambda i: (i, 0))]
    seq_tiles = tiles_per_batch - 1

    def x_map(i):
        return ((i // tiles_per_batch) * seq_tiles + jnp.maximum(i % tiles_per_batch - 1, 0), 0)

    return [pl.BlockSpec((TM, D_MODEL), x_map), _resident((N_META, D_MODEL))]


def _inproj_kernel(*refs, tiles_per_batch, first):
    n_hidden = 2 if first else 1
    hidden = refs[:n_hidden]
    (n1_ref, w_ref, cq_ref, csc_ref, alog_ref, dtb_ref,
     qkv_ref, zs_ref, gates_ref, gatest_ref, sgg_ref,
     halo_qkv_ref, halo_sc_ref) = refs[n_hidden:]
    tile_in_batch = pl.program_id(0) % tiles_per_batch
    tm = qkv_ref.shape[0]

    @pl.when(tile_in_batch == 0)
    def _():
        halo_qkv_ref[...] = jnp.zeros_like(halo_qkv_ref)
        halo_sc_ref[...] = jnp.zeros_like(halo_sc_ref)

    x = _input_tile(*hidden, tile_in_batch) if first else hidden[0][...]
    xn = _rms(x, n1_ref[...]).astype(BF16)

    def proj(off, width=D_MODEL):
        return jnp.dot(xn, w_ref[:, off:off + width], preferred_element_type=F32)

    def causal_conv(raw, halo_ref, col, taps_ref, n_taps):
        width = raw.shape[1]
        ext = jnp.concatenate([halo_ref[:, col:col + width], raw], axis=0)
        halo_ref[:, col:col + width] = raw[tm - SUBLANES:tm, :]
        taps = [taps_ref[j:j + 1, col:col + width] for j in range(n_taps)]
        if n_taps == 4:
            delayed = jnp.concatenate([ext[0:1], ext[0:SUBLANES + tm - 1]], axis=0)
            older = ext * taps[1] + delayed * taps[0]
            return (raw * taps[3] + delayed[SUBLANES:] * taps[2]
                    + older[SUBLANES - 2:SUBLANES - 2 + tm])
        acc = raw * taps[n_taps - 1]
        for j in range(n_taps - 1):
            start = SUBLANES - (n_taps - 1) + j
            acc = acc + ext[start:start + tm, :] * taps[j]
        return acc

    def head_l2norm(y, scale):
        outs = []
        for hd in range(y.shape[1] // DN_HEAD_DIM):
            yh = y[:, hd * DN_HEAD_DIM:(hd + 1) * DN_HEAD_DIM]
            inv = lax.rsqrt(jnp.sum(yh * yh, axis=-1, keepdims=True) + EPS)
            outs.append(yh * (inv * scale))
        return jnp.concatenate(outs, axis=-1)

    def after(x, anchor):
        bits = pltpu.bitcast(anchor[0:1, :], jnp.int32)
        zero = lax.shift_right_logical(lax.shift_right_logical(bits, 16), 16).astype(F32)
        return x + zero

    conv_q = causal_conv(proj(OFF_Q), halo_qkv_ref, 0, cq_ref, DN_CONV)
    conv_k = causal_conv(proj(OFF_K), halo_qkv_ref, D_MODEL, cq_ref, DN_CONV)
    conv_v = causal_conv(proj(OFF_V), halo_qkv_ref, 2 * D_MODEL, cq_ref, DN_CONV)
    zs_ref[...] = _silu(proj(OFF_Z)).astype(BF16)

    raw_c, raw_u, raw_b = proj(OFF_C), proj(OFF_U), proj(OFF_B)
    sgg_ref[:, 0:D_MODEL] = (
        raw_b * causal_conv(raw_c * raw_u, halo_sc_ref, 0, csc_ref, SC_CONV)).astype(BF16)
    raw_ga, raw_gb = proj(OFF_GA), proj(OFF_GB)
    sgg_ref[:, D_MODEL:2 * D_MODEL] = jax.nn.sigmoid(raw_ga).astype(BF16)
    sgg_ref[:, 2 * D_MODEL:3 * D_MODEL] = jax.nn.sigmoid(raw_gb).astype(BF16)

    qkv_ref[:, 0:D_MODEL] = head_l2norm(
        _silu(after(conv_q, raw_u)), DN_HEAD_DIM ** -0.5).astype(BF16)
    qkv_ref[:, D_MODEL:2 * D_MODEL] = head_l2norm(_silu(after(conv_k, raw_b)), 1.0).astype(BF16)
    qkv_ref[:, 2 * D_MODEL:3 * D_MODEL] = _silu(after(conv_v, raw_ga)).astype(BF16)

    ba = proj(OFF_BA, LANES)
    lane = lax.broadcasted_iota(jnp.int32, ba.shape, 1)
    g = -jnp.exp(alog_ref[...]) * jax.nn.softplus(ba + dtb_ref[...])
    gates = jnp.where(lane < DN_HEADS, jax.nn.sigmoid(ba), jnp.where(lane < 2 * DN_HEADS, g, 0.0))
    gates_ref[...] = gates
    gates_t = gates.T
    for c in range(tm // CHUNK):
        gatest_ref[c] = gates_t[0:2 * DN_HEADS, c * CHUNK:(c + 1) * CHUNK]


def _inproj(hidden, layer, n1, w_packed, cq, csc, alog_rows, dtb_rows, rows, tiles_per_batch):
    first = len(hidden) == 2

    def rows_of(width):
        return pl.BlockSpec((TM, width), lambda i: (i, 0))

    return pl.pallas_call(
        functools.partial(_inproj_kernel, tiles_per_batch=tiles_per_batch, first=first),
        grid=(rows // TM,),
        in_specs=_hidden_specs(first, tiles_per_batch) + [
            _layer_block(layer, (1, D_MODEL)),
            _layer_block(layer, (D_MODEL, PACKED_WIDTH)),
            _layer_block(layer, (DN_CONV, 3 * D_MODEL)),
            _layer_block(layer, (SC_CONV, D_MODEL)),
            _layer_block(layer, (1, LANES)),
            _layer_block(layer, (1, LANES)),
        ],
        out_specs=[
            rows_of(3 * D_MODEL), rows_of(D_MODEL), rows_of(LANES),
            pl.BlockSpec((TM // CHUNK, 2 * DN_HEADS, CHUNK), lambda i: (i, 0, 0)),
            rows_of(3 * D_MODEL),
        ],
        out_shape=[
            jax.ShapeDtypeStruct((rows, 3 * D_MODEL), BF16),
            jax.ShapeDtypeStruct((rows, D_MODEL), BF16),
            jax.ShapeDtypeStruct((rows, LANES), F32),
            jax.ShapeDtypeStruct((rows // CHUNK, 2 * DN_HEADS, CHUNK), F32),
            jax.ShapeDtypeStruct((rows, 3 * D_MODEL), BF16),
        ],
        scratch_shapes=[
            pltpu.VMEM((SUBLANES, 3 * D_MODEL), F32),
            pltpu.VMEM((SUBLANES, D_MODEL), F32),
        ],
        compiler_params=pltpu.CompilerParams(
            dimension_semantics=("arbitrary",), vmem_limit_bytes=VMEM_LIMIT),
        name="inproj",
    )(*hidden, n1, w_packed, cq, csc, alog_rows, dtb_rows)


def _split3(x):
    x1 = x.astype(BF16)
    r1 = x - x1.astype(F32)
    x2 = r1.astype(BF16)
    x3 = (r1 - x2.astype(F32)).astype(BF16)
    return x1, x2, x3


def _delta_kernel(q_ref, k_ref, v_ref, zs_ref, gates_ref, gatest_ref, onorm_ref, og_ref,
                  state_ref, lhs1_ref, lhs2_ref, u_ref, gl_ref):
    @pl.when(pl.program_id(0) == 0)
    def _():
        state_ref[...] = jnp.zeros_like(state_ref)

    bsz, tc, _ = q_ref.shape
    n_chunks = tc // CHUNK
    n_slots = bsz * n_chunks
    row = lax.broadcasted_iota(jnp.int32, (CHUNK, 2 * CHUNK), 0)
    lane = lax.broadcasted_iota(jnp.int32, (CHUNK, 2 * CHUNK), 1)
    col = lane & (CHUNK - 1)
    left = lane < CHUNK
    causal = row >= col
    strict = row > col
    same16 = (row // INV_BLOCK) == (col // INV_BLOCK)
    eye = jnp.where(row == col, 1.0, 0.0)
    tril = jnp.where(causal[:, 0:CHUNK], 1.0, 0.0).astype(BF16)
    triu_l = jnp.where(left & (row <= col), 1.0, 0.0).astype(BF16)
    triu_r = jnp.where(jnp.logical_not(left) & (row <= col), 1.0, 0.0).astype(BF16)
    zeros_bf = jnp.zeros((CHUNK, DN_HEAD_DIM), BF16)
    zeros_f = jnp.zeros((CHUNK, DN_HEAD_DIM), F32)
    onorm = onorm_ref[...]
    problems = [(c, hd) for c in range(n_slots) for hd in range(DN_HEADS)]
    pairs = [(c, j) for c in range(n_slots) for j in range(DN_HEADS // 2)]
    nt_dims = (((1,), (1,)), ((), ()))

    def tokens(slot):
        return slot // n_chunks, slice((slot % n_chunks) * CHUNK, (slot % n_chunks + 1) * CHUNK)

    def head_tile(slot, hd):
        return tokens(slot) + (slice(hd * DN_HEAD_DIM, (hd + 1) * DN_HEAD_DIM),)

    def dot3(a_parts, b):
        return sum(jnp.dot(a, b, preferred_element_type=F32) for a in a_parts)

    def half_pad(x, hd, zeros):
        return jnp.concatenate([x, zeros] if hd % 2 == 0 else [zeros, x], axis=0)

    def blockdiag(y):
        return jnp.concatenate([jnp.where(left, y, 0.0), jnp.where(left, 0.0, y)], axis=0).astype(BF16)

    gates, gcol, grow_l, grow_r = [], [], [], []
    for c in range(n_slots):
        gt = gates_ref[tokens(c) + (slice(None),)]
        gates.append(gt)
        g_parts = _split3(gt)
        gcol.append(sum(jnp.dot(tril, gp, preferred_element_type=F32) for gp in g_parts))
        t_parts = _split3(gatest_ref[c // n_chunks, c % n_chunks])
        grow_l.append(dot3(t_parts, triu_l))
        grow_r.append(dot3(t_parts, triu_r))

    gc_b, kq_lhs, k_bf, rhs = {}, {}, {}, {}
    for p in problems:
        c, hd = p
        q = q_ref[head_tile(c, hd)].astype(F32)
        k_bf[p] = k_ref[head_tile(c, hd)]
        k = k_bf[p].astype(F32)
        v = v_ref[head_tile(c, hd)].astype(F32)
        beta = jnp.broadcast_to(gates[c][:, hd:hd + 1], (CHUNK, DN_HEAD_DIM))
        gc = jnp.broadcast_to(gcol[c][:, DN_HEADS + hd:DN_HEADS + hd + 1], (CHUNK, DN_HEAD_DIM))
        gc_b[p] = gc
        g_last = gc[CHUNK - 1:CHUNK, :]
        e_gc = jnp.exp(gc)
        kb = k * beta
        kdec = k * jnp.exp(g_last - gc)
        kq_lhs[p] = jnp.concatenate([kb, q], axis=0).astype(BF16)
        rhs[p] = half_pad(jnp.concatenate([v * beta, kb * e_gc], axis=1).astype(BF16), hd,
                          jnp.zeros((CHUNK, 2 * DN_HEAD_DIM), BF16))
        lhs1_ref[c, hd, CHUNK:2 * CHUNK, :] = (q * e_gc).astype(BF16)
        lhs2_ref[c, hd, CHUNK:CHUNK + DN_HEAD_DIM, :] = half_pad(kdec, hd, zeros_f).T.astype(BF16)
        gl_ref[c * DN_HEADS + hd:c * DN_HEADS + hd + 1, :] = jnp.exp(g_last)

    def level(fn):
        return {p: fn(p) for p in pairs}

    def heads_of(p):
        return (p[0], 2 * p[1]), (p[0], 2 * p[1] + 1)

    def pair_decay(p):
        h0, h1 = heads_of(p)
        c = p[0]
        grow = (grow_l[c][DN_HEADS + h0[1]:DN_HEADS + h0[1] + 1, :]
                + grow_r[c][DN_HEADS + h1[1]:DN_HEADS + h1[1] + 1, :])
        gc = jnp.where(left, gc_b[h0], gc_b[h1])
        return jnp.exp(jnp.where(causal, gc - grow, -jnp.inf))

    def pair_kq(p):
        h0, h1 = heads_of(p)
        keys = jnp.concatenate([k_bf[h0], k_bf[h1]], axis=0)
        both = lax.dot_general(jnp.concatenate([kq_lhs[h0], kq_lhs[h1]], axis=0), keys, nt_dims,
                               preferred_element_type=F32)
        return jnp.where(jnp.concatenate([left, left], axis=0),
                         both[0:2 * CHUNK], both[2 * CHUNK:4 * CHUNK])

    decay = level(pair_decay)
    kq = level(pair_kq)
    a_mat = level(lambda p: jnp.where(strict, kq[p][0:CHUNK] * decay[p], 0.0))
    for p in pairs:
        qk = (kq[p][CHUNK:2 * CHUNK] * decay[p]).astype(BF16)
        for c, hd in heads_of(p):
            lhs2_ref[c, hd, 0:CHUNK, :] = qk

    def pdot(x, y_bd):
        return jnp.dot(x.astype(BF16), y_bd, preferred_element_type=F32)

    n1 = level(lambda p: jnp.where(same16, -a_mat[p], 0.0))
    n2 = level(lambda p: pdot(n1[p], blockdiag(n1[p])))
    t = level(lambda p: eye + n1[p])
    tn = level(lambda p: pdot(jnp.concatenate([t[p], n2[p]], axis=0), blockdiag(n2[p])))
    t = level(lambda p: t[p] + tn[p][0:CHUNK])
    n4 = level(lambda p: tn[p][CHUNK:2 * CHUNK])
    tn = level(lambda p: pdot(jnp.concatenate([t[p], n4[p]], axis=0), blockdiag(n4[p])))
    t = level(lambda p: t[p] + tn[p][0:CHUNK])
    t = level(lambda p: t[p] + pdot(t[p], blockdiag(tn[p][CHUNK:2 * CHUNK])))
    off = level(lambda p: jnp.where(same16, 0.0, a_mat[p]))
    m = level(lambda p: -pdot(off[p], blockdiag(t[p])))
    tm_ = level(lambda p: pdot(jnp.concatenate([t[p], m[p]], axis=0), blockdiag(m[p])))
    t = level(lambda p: t[p] + tm_[p][0:CHUNK])
    t = level(lambda p: (t[p] + pdot(t[p], blockdiag(tm_[p][CHUNK:2 * CHUNK]))).astype(BF16))
    for p in pairs:
        for c, hd in heads_of(p):
            uw = jnp.dot(t[p], rhs[(c, hd)], preferred_element_type=F32)
            u_ref[c, hd] = uw[:, 0:DN_HEAD_DIM]
            lhs1_ref[c, hd, 0:CHUNK, :] = uw[:, DN_HEAD_DIM:2 * DN_HEAD_DIM].astype(BF16)

    chains = [(b, hd) for b in range(bsz) for hd in range(DN_HEADS)]
    state = {ch: state_ref[ch[0] * DN_HEADS + ch[1]] for ch in chains}
    for c in range(n_chunks):
        r1, r2 = {}, {}
        for b, hd in chains:
            r1[b, hd] = jnp.dot(lhs1_ref[b * n_chunks + c, hd], state[b, hd].astype(BF16),
                                preferred_element_type=F32)
        for b, hd in chains:
            slot = b * n_chunks + c
            v_new = half_pad((u_ref[slot, hd] - r1[b, hd][0:CHUNK]).astype(BF16), hd, zeros_bf)
            r2[b, hd] = jnp.dot(lhs2_ref[slot, hd], v_new, preferred_element_type=F32)
        for b, hd in chains:
            slot = b * n_chunks + c
            gl = gl_ref[slot * DN_HEADS + hd:slot * DN_HEADS + hd + 1, :]
            state[b, hd] = state[b, hd] * gl + r2[b, hd][CHUNK:CHUNK + DN_HEAD_DIM]
            o = r1[b, hd][CHUNK:2 * CHUNK] + r2[b, hd][0:CHUNK]
            og_ref[head_tile(slot, hd)] = (
                _rms(o, onorm) * zs_ref[head_tile(slot, hd)].astype(F32)).astype(BF16)
    for b, hd in chains:
        state_ref[b * DN_HEADS + hd] = state[b, hd]


def _delta(qkv, zs, gates, gatest, layer, onorm, bsz):
    rows = qkv.shape[0]
    lp = rows // bsz
    n_chunks = TC // CHUNK
    n_slots = bsz * n_chunks

    def per_batch(a):
        return a.reshape((bsz, a.shape[0] // bsz) + a.shape[1:])

    def column_block(j):
        return pl.BlockSpec((bsz, TC, D_MODEL), lambda i: (0, i, j))

    row_spec = column_block(0)
    qkv = per_batch(qkv)
    og = pl.pallas_call(
        _delta_kernel,
        grid=(lp // TC,),
        in_specs=[
            column_block(0), column_block(1), column_block(2), row_spec,
            pl.BlockSpec((bsz, TC, LANES), lambda i: (0, i, 0)),
            pl.BlockSpec((bsz, n_chunks, 2 * DN_HEADS, CHUNK), lambda i: (0, i, 0, 0)),
            _layer_block(layer, (1, DN_HEAD_DIM)),
        ],
        out_specs=row_spec,
        out_shape=jax.ShapeDtypeStruct((bsz, lp, D_MODEL), BF16),
        scratch_shapes=[
            pltpu.VMEM((bsz * DN_HEADS, DN_HEAD_DIM, DN_HEAD_DIM), F32),
            pltpu.VMEM((n_slots, DN_HEADS, 2 * CHUNK, DN_HEAD_DIM), BF16),
            pltpu.VMEM((n_slots, DN_HEADS, CHUNK + DN_HEAD_DIM, 2 * CHUNK), BF16),
            pltpu.VMEM((n_slots, DN_HEADS, CHUNK, DN_HEAD_DIM), F32),
            pltpu.VMEM((n_slots * DN_HEADS, DN_HEAD_DIM), F32),
        ],
        compiler_params=pltpu.CompilerParams(
            dimension_semantics=("arbitrary",), vmem_limit_bytes=VMEM_LIMIT),
        name="delta_rule",
    )(qkv, qkv, qkv, *(per_batch(a) for a in (zs, gates, gatest)), onorm)
    return og.reshape(rows, D_MODEL)


FF_SPLITS = ((0, D_MODEL), (D_MODEL, 2 * D_MODEL), (2 * D_MODEL, D_FF))


def _ffn_kernel(*refs, tiles_per_batch, first, final):
    n_hidden = 2 if first else 1
    hidden = refs[:n_hidden]
    (og_ref, s_ref, ga_ref, gb_ref, wdn_ref, wsc_ref, wo_ref, n2_ref, wgu_ref, wdown_ref,
     fnorm_ref, out_ref) = refs[n_hidden:]
    tile_in_batch = pl.program_id(0) % tiles_per_batch

    def body():
        h = _input_tile(*hidden, tile_in_batch) if first else hidden[0][...]
        ya = jnp.dot(og_ref[...], wdn_ref[...].astype(BF16), preferred_element_type=F32)
        yb = jnp.dot(s_ref[...], wsc_ref[...].astype(BF16), preferred_element_type=F32)
        mixed = ga_ref[...].astype(F32) * ya + gb_ref[...].astype(F32) * yb
        h1 = h + _bdot(mixed, wo_ref[...])
        hn = _rms(h1, n2_ref[...]).astype(BF16)
        acc = h1
        for lo, hi in FF_SPLITS:
            gate = jnp.dot(hn, wgu_ref[:, lo:hi], preferred_element_type=F32)
            up = jnp.dot(hn, wgu_ref[:, D_FF + lo:D_FF + hi], preferred_element_type=F32)
            acc = acc + _bdot(_silu(gate) * up, wdown_ref[lo:hi, :])
        out_ref[...] = _rms(acc, fnorm_ref[...]) if final else acc

    if final:
        pl.when(tile_in_batch != 0)(body)
    else:
        body()


def _ffn(hidden, og, sgg, layer, wdn, wsc, wo, n2, wgu, wdown, fnorm, rows, tiles_per_batch,
         final):
    first = len(hidden) == 2
    row_spec = pl.BlockSpec((TM, D_MODEL), lambda i: (i, 0))

    def sgg_block(j):
        return pl.BlockSpec((TM, D_MODEL), lambda i: (i, j))

    if final:
        seq_tiles = tiles_per_batch - 1
        out_rows = rows // tiles_per_batch * seq_tiles
        out_spec = pl.BlockSpec(
            (TM, D_MODEL),
            lambda i: ((i // tiles_per_batch) * seq_tiles + jnp.maximum(i % tiles_per_batch - 1, 0), 0))
    else:
        out_rows, out_spec = rows, row_spec
    return pl.pallas_call(
        functools.partial(_ffn_kernel, tiles_per_batch=tiles_per_batch, first=first, final=final),
        grid=(rows // TM,),
        in_specs=_hidden_specs(first, tiles_per_batch) + [
            row_spec, sgg_block(0), sgg_block(1), sgg_block(2),
            _layer_block(layer, (D_MODEL, D_MODEL)),
            _layer_block(layer, (D_MODEL, D_MODEL)),
            _layer_block(layer, (D_MODEL, D_MODEL)),
            _layer_block(layer, (1, D_MODEL)),
            _layer_block(layer, (D_MODEL, 2 * D_FF)),
            _layer_block(layer, (D_FF, D_MODEL)),
            _resident((1, D_MODEL)),
        ],
        out_specs=out_spec,
        out_shape=jax.ShapeDtypeStruct((out_rows, D_MODEL), F32),
        compiler_params=pltpu.CompilerParams(
            dimension_semantics=("arbitrary",), vmem_limit_bytes=VMEM_LIMIT),
        name="mix_ffn",
    )(*hidden, og, sgg, sgg, sgg, wdn, wsc, wo, n2, wgu, wdown, fnorm)


def _pack_kernel(wt_ref, out_ref):
    g0, g1 = W_IN_GATES.start, W_IN_GATES.stop
    out_ref[:, 0:g0] = wt_ref[0:g0, :].T.astype(BF16)
    gates = wt_ref[g0:g0 + LANES, :].T
    lane = lax.broadcasted_iota(jnp.int32, gates.shape, 1)
    out_ref[:, g0:g0 + LANES] = jnp.where(lane < g1 - g0, gates, 0.0).astype(BF16)
    out_ref[:, g0 + LANES:] = wt_ref[g1:, :].T.astype(BF16)


def _pack_w_in(w_in):
    depth, d, width = w_in.shape
    return pl.pallas_call(
        _pack_kernel,
        grid=(depth, d // LANES),
        in_specs=[pl.BlockSpec((None, width, LANES), lambda l, i: (l, 0, i))],
        out_specs=pl.BlockSpec((None, LANES, PACKED_WIDTH), lambda l, i: (l, i, 0)),
        out_shape=jax.ShapeDtypeStruct((depth, d, PACKED_WIDTH), BF16),
        compiler_params=pltpu.CompilerParams(
            dimension_semantics=("arbitrary", "arbitrary"), vmem_limit_bytes=VMEM_LIMIT),
        name="pack_w_in",
    )(jnp.swapaxes(w_in, 1, 2))


def _gate_rows(p):
    return jnp.pad(p.astype(F32), ((0, 0), (DN_HEADS, LANES - 2 * DN_HEADS)))[:, None, :]


def kernel(x, meta_tokens, norm1, w_in, conv_qkv, a_log, dt_bias, o_norm, w_dn_out, conv_sc,
           w_sc_out, w_o, norm2, w_gate_up, w_down, final_norm):
    bsz, seq, d = x.shape
    depth = w_in.shape[0]
    assert d == D_MODEL and seq % TM == 0 and TM % TC == 0 and TM >= N_META
    tiles_per_batch = seq // TM + 1
    rows = bsz * tiles_per_batch * TM

    w_packed = _pack_w_in(w_in)
    wdn, wsc, wo = w_dn_out, w_sc_out, w_o
    wgu, wdown = w_gate_up.astype(BF16), w_down
    n1, n2, onorm = norm1[:, None, :], norm2[:, None, :], o_norm[:, None, :]
    alog_rows, dtb_rows = _gate_rows(a_log), _gate_rows(dt_bias)
    fnorm = final_norm.reshape(1, d)

    hidden = (x.reshape(bsz * seq, d), meta_tokens.astype(x.dtype))
    for layer in range(depth):
        qkv, zs, gates, gatest, sgg = _inproj(
            hidden, layer, n1, w_packed, conv_qkv, conv_sc, alog_rows, dtb_rows,
            rows, tiles_per_batch)
        og = _delta(qkv, zs, gates, gatest, layer, onorm, bsz)
        h = _ffn(hidden, og, sgg, layer, wdn, wsc, wo, n2, wgu, wdown, fnorm,
                 rows, tiles_per_batch, layer == depth - 1)
        hidden = (h,)
    return hidden[0].reshape(bsz, seq, d)
```

```python
import functools

import jax
import jax.numpy as jnp
from jax import lax
from jax.experimental import pallas as pl
from jax.experimental.pallas import tpu as pltpu

D_MODEL = 1024
N_META = 16
DN_HEADS = 8
DN_HEAD_DIM = 128
DN_CONV = 4
SC_CONV = 3
CHUNK = 64
D_FF = 2816
EPS = 1e-6

LANES = 128
SUBLANES = 8
TM = 256
TC = 384
VMEM_LIMIT = 56 * 2 ** 20
INV_BLOCK = 16

W_IN_GATES = slice(4 * D_MODEL, 4 * D_MODEL + 2 * DN_HEADS)
OFF_Q, OFF_K, OFF_V, OFF_Z = (j * D_MODEL for j in range(4))
OFF_BA = 4 * D_MODEL
OFF_C, OFF_U, OFF_B, OFF_GA, OFF_GB = (OFF_BA + LANES + j * D_MODEL for j in range(5))
PACKED_WIDTH = OFF_GB + D_MODEL

F32 = jnp.float32
BF16 = jnp.bfloat16


def _bdot(a, b):
    return jnp.dot(a.astype(BF16), b.astype(BF16), preferred_element_type=F32)


def _silu(x):
    return x * jax.nn.sigmoid(x)


def _rms(x, gain):
    return x * lax.rsqrt(jnp.mean(x * x, axis=-1, keepdims=True) + EPS) * gain


def _layer_block(layer, shape):
    return pl.BlockSpec((None,) + shape, lambda *_: (layer,) + (0,) * len(shape),
                        pipeline_mode=pl.Buffered(1))


def _resident(shape):
    return pl.BlockSpec(shape, lambda *_: (0,) * len(shape), pipeline_mode=pl.Buffered(1))


def _input_tile(x_ref, meta_ref, tile_in_batch):
    tm, d = x_ref.shape
    front = jnp.concatenate([jnp.zeros((tm - N_META, d), F32), meta_ref[...]], axis=0)
    return jnp.where(tile_in_batch == 0, front, x_ref[...])


def _hidden_specs(first, tiles_per_batch):
    if not first:
        return [pl.BlockSpec((TM, D_MODEL), lambda i: (i, 0))]
    seq_tiles = tiles_per_batch - 1

    def x_map(i):
        return ((i // tiles_per_batch) * seq_tiles + jnp.maximum(i % tiles_per_batch - 1, 0), 0)

    return [pl.BlockSpec((TM, D_MODEL), x_map), _resident((N_META, D_MODEL))]


def _inproj_kernel(*refs, tiles_per_batch, first):
    n_hidden = 2 if first else 1
    hidden = refs[:n_hidden]
    (n1_ref, w_ref, cq_ref, csc_ref, alog_ref, dtb_ref,
     qkv_ref, zs_ref, gates_ref, gatest_ref, sgg_ref,
     halo_qkv_ref, halo_sc_ref) = refs[n_hidden:]
    tile_in_batch = pl.program_id(0) % tiles_per_batch
    tm = qkv_ref.shape[0]

    @pl.when(tile_in_batch == 0)
    def _():
        halo_qkv_ref[...] = jnp.zeros_like(halo_qkv_ref)
        halo_sc_ref[...] = jnp.zeros_like(halo_sc_ref)

    x = _input_tile(*hidden, tile_in_batch) if first else hidden[0][...]
    xn = _rms(x, n1_ref[...]).astype(BF16)

    def proj(off, width=D_MODEL):
        return jnp.dot(xn, w_ref[:, off:off + width], preferred_element_type=F32)

    def causal_conv(raw, halo_ref, col, taps_ref, n_taps):
        width = raw.shape[1]
        ext = jnp.concatenate([halo_ref[:, col:col + width], raw], axis=0)
        halo_ref[:, col:col + width] = raw[tm - SUBLANES:tm, :]
        taps = [taps_ref[j:j + 1, col:col + width] for j in range(n_taps)]
        if n_taps == 4:
            delayed = jnp.concatenate([ext[0:1], ext[0:SUBLANES + tm - 1]], axis=0)
            older = ext * taps[1] + delayed * taps[0]
            return (raw * taps[3] + delayed[SUBLANES:] * taps[2]
                    + older[SUBLANES - 2:SUBLANES - 2 + tm])
        acc = raw * taps[n_taps - 1]
        for j in range(n_taps - 1):
            start = SUBLANES - (n_taps - 1) + j
            acc = acc + ext[start:start + tm, :] * taps[j]
        return acc

    def head_l2norm(y, scale):
        outs = []
        for hd in range(y.shape[1] // DN_HEAD_DIM):
            yh = y[:, hd * DN_HEAD_DIM:(hd + 1) * DN_HEAD_DIM]
            inv = lax.rsqrt(jnp.sum(yh * yh, axis=-1, keepdims=True) + EPS)
            outs.append(yh * (inv * scale))
        return jnp.concatenate(outs, axis=-1)

    yq = _silu(causal_conv(proj(OFF_Q), halo_qkv_ref, 0, cq_ref, DN_CONV))
    qkv_ref[:, 0:D_MODEL] = head_l2norm(yq, DN_HEAD_DIM ** -0.5).astype(BF16)
    yk = _silu(causal_conv(proj(OFF_K), halo_qkv_ref, D_MODEL, cq_ref, DN_CONV))
    qkv_ref[:, D_MODEL:2 * D_MODEL] = head_l2norm(yk, 1.0).astype(BF16)
    yv = _silu(causal_conv(proj(OFF_V), halo_qkv_ref, 2 * D_MODEL, cq_ref, DN_CONV))
    qkv_ref[:, 2 * D_MODEL:3 * D_MODEL] = yv.astype(BF16)
    zs_ref[...] = _silu(proj(OFF_Z)).astype(BF16)

    cu = proj(OFF_C) * proj(OFF_U)
    sgg_ref[:, 0:D_MODEL] = (
        proj(OFF_B) * causal_conv(cu, halo_sc_ref, 0, csc_ref, SC_CONV)).astype(BF16)
    sgg_ref[:, D_MODEL:2 * D_MODEL] = jax.nn.sigmoid(proj(OFF_GA)).astype(BF16)
    sgg_ref[:, 2 * D_MODEL:3 * D_MODEL] = jax.nn.sigmoid(proj(OFF_GB)).astype(BF16)

    ba = proj(OFF_BA, LANES)
    lane = lax.broadcasted_iota(jnp.int32, ba.shape, 1)
    g = -jnp.exp(alog_ref[...]) * jax.nn.softplus(ba + dtb_ref[...])
    gates = jnp.where(lane < DN_HEADS, jax.nn.sigmoid(ba), jnp.where(lane < 2 * DN_HEADS, g, 0.0))
    gates_ref[...] = gates
    gates_t = gates.T
    for c in range(tm // CHUNK):
        gatest_ref[c] = gates_t[0:2 * DN_HEADS, c * CHUNK:(c + 1) * CHUNK]


def _inproj(hidden, layer, n1, w_packed, cq, csc, alog_rows, dtb_rows, rows, tiles_per_batch):
    first = len(hidden) == 2

    def rows_of(width):
        return pl.BlockSpec((TM, width), lambda i: (i, 0))

    return pl.pallas_call(
        functools.partial(_inproj_kernel, tiles_per_batch=tiles_per_batch, first=first),
        grid=(rows // TM,),
        in_specs=_hidden_specs(first, tiles_per_batch) + [
            _layer_block(layer, (1, D_MODEL)),
            _layer_block(layer, (D_MODEL, PACKED_WIDTH)),
            _layer_block(layer, (DN_CONV, 3 * D_MODEL)),
            _layer_block(layer, (SC_CONV, D_MODEL)),
            _layer_block(layer, (1, LANES)),
            _layer_block(layer, (1, LANES)),
        ],
        out_specs=[
            rows_of(3 * D_MODEL), rows_of(D_MODEL), rows_of(LANES),
            pl.BlockSpec((TM // CHUNK, 2 * DN_HEADS, CHUNK), lambda i: (i, 0, 0)),
            rows_of(3 * D_MODEL),
        ],
        out_shape=[
            jax.ShapeDtypeStruct((rows, 3 * D_MODEL), BF16),
            jax.ShapeDtypeStruct((rows, D_MODEL), BF16),
            jax.ShapeDtypeStruct((rows, LANES), F32),
            jax.ShapeDtypeStruct((rows // CHUNK, 2 * DN_HEADS, CHUNK), F32),
            jax.ShapeDtypeStruct((rows, 3 * D_MODEL), BF16),
        ],
        scratch_shapes=[
            pltpu.VMEM((SUBLANES, 3 * D_MODEL), F32),
            pltpu.VMEM((SUBLANES, D_MODEL), F32),
        ],
        compiler_params=pltpu.CompilerParams(
            dimension_semantics=("arbitrary",), vmem_limit_bytes=VMEM_LIMIT),
        name="inproj",
    )(*hidden, n1, w_packed, cq, csc, alog_rows, dtb_rows)


def _split3(x):
    x1 = x.astype(BF16)
    r1 = x - x1.astype(F32)
    x2 = r1.astype(BF16)
    x3 = (r1 - x2.astype(F32)).astype(BF16)
    return x1, x2, x3


def _delta_kernel(q_ref, k_ref, v_ref, zs_ref, gates_ref, gatest_ref, onorm_ref, og_ref,
                  state_ref, lhs1_ref, lhs2_ref, u_ref, gl_ref):
    @pl.when(pl.program_id(0) == 0)
    def _():
        state_ref[...] = jnp.zeros_like(state_ref)

    bsz, tc, _ = q_ref.shape
    n_chunks = tc // CHUNK
    n_slots = bsz * n_chunks
    row = lax.broadcasted_iota(jnp.int32, (CHUNK, 2 * CHUNK), 0)
    lane = lax.broadcasted_iota(jnp.int32, (CHUNK, 2 * CHUNK), 1)
    col = lane & (CHUNK - 1)
    left = lane < CHUNK
    causal = row >= col
    strict = row > col
    same16 = (row // INV_BLOCK) == (col // INV_BLOCK)
    eye = jnp.where(row == col, 1.0, 0.0)
    tril = jnp.where(causal[:, 0:CHUNK], 1.0, 0.0).astype(BF16)
    triu_l = jnp.where(left & (row <= col), 1.0, 0.0).astype(BF16)
    triu_r = jnp.where(jnp.logical_not(left) & (row <= col), 1.0, 0.0).astype(BF16)
    zeros_bf = jnp.zeros((CHUNK, DN_HEAD_DIM), BF16)
    zeros_f = jnp.zeros((CHUNK, DN_HEAD_DIM), F32)
    onorm = onorm_ref[...]
    problems = [(c, hd) for c in range(n_slots) for hd in range(DN_HEADS)]
    pairs = [(c, j) for c in range(n_slots) for j in range(DN_HEADS // 2)]
    nt_dims = (((1,), (1,)), ((), ()))

    def tokens(slot):
        return slot // n_chunks, slice((slot % n_chunks) * CHUNK, (slot % n_chunks + 1) * CHUNK)

    def head_tile(slot, hd):
        return tokens(slot) + (slice(hd * DN_HEAD_DIM, (hd + 1) * DN_HEAD_DIM),)

    def dot3(a_parts, b):
        return sum(jnp.dot(a, b, preferred_element_type=F32) for a in a_parts)

    def half_pad(x, hd, zeros):
        return jnp.concatenate([x, zeros] if hd % 2 == 0 else [zeros, x], axis=0)

    def blockdiag(y):
        return jnp.concatenate([jnp.where(left, y, 0.0), jnp.where(left, 0.0, y)], axis=0).astype(BF16)

    gates, gcol, grow_l, grow_r = [], [], [], []
    for c in range(n_slots):
        gt = gates_ref[tokens(c) + (slice(None),)]
        gates.append(gt)
        g_parts = _split3(gt)
        gcol.append(sum(jnp.dot(tril, gp, preferred_element_type=F32) for gp in g_parts))
        t_parts = _split3(gatest_ref[c // n_chunks, c % n_chunks])
        grow_l.append(dot3(t_parts, triu_l))
        grow_r.append(dot3(t_parts, triu_r))

    gc_b, kq_lhs, k_bf, rhs = {}, {}, {}, {}
    for p in problems:
        c, hd = p
        q = q_ref[head_tile(c, hd)].astype(F32)
        k_bf[p] = k_ref[head_tile(c, hd)]
        k = k_bf[p].astype(F32)
        v = v_ref[head_tile(c, hd)].astype(F32)
        beta = jnp.broadcast_to(gates[c][:, hd:hd + 1], (CHUNK, DN_HEAD_DIM))
        gc = jnp.broadcast_to(gcol[c][:, DN_HEADS + hd:DN_HEADS + hd + 1], (CHUNK, DN_HEAD_DIM))
        gc_b[p] = gc
        g_last = gc[CHUNK - 1:CHUNK, :]
        e_gc = jnp.exp(gc)
        kb = k * beta
        kdec = k * jnp.exp(g_last - gc)
        kq_lhs[p] = jnp.concatenate([kb, q], axis=0).astype(BF16)
        rhs[p] = half_pad(jnp.concatenate([v * beta, kb * e_gc], axis=1).astype(BF16), hd,
                          jnp.zeros((CHUNK, 2 * DN_HEAD_DIM), BF16))
        lhs1_ref[c, hd, CHUNK:2 * CHUNK, :] = (q * e_gc).astype(BF16)
        lhs2_ref[c, hd, CHUNK:CHUNK + DN_HEAD_DIM, :] = half_pad(kdec, hd, zeros_f).T.astype(BF16)
        gl_ref[c * DN_HEADS + hd:c * DN_HEADS + hd + 1, :] = jnp.exp(g_last)

    def level(fn):
        return {p: fn(p) for p in pairs}

    def heads_of(p):
        return (p[0], 2 * p[1]), (p[0], 2 * p[1] + 1)

    def pair_decay(p):
        h0, h1 = heads_of(p)
        c = p[0]
        grow = (grow_l[c][DN_HEADS + h0[1]:DN_HEADS + h0[1] + 1, :]
                + grow_r[c][DN_HEADS + h1[1]:DN_HEADS + h1[1] + 1, :])
        gc = jnp.where(left, gc_b[h0], gc_b[h1])
        return jnp.exp(jnp.where(causal, gc - grow, -jnp.inf))

    def pair_kq(p):
        h0, h1 = heads_of(p)
        keys = jnp.concatenate([k_bf[h0], k_bf[h1]], axis=0)
        both = lax.dot_general(jnp.concatenate([kq_lhs[h0], kq_lhs[h1]], axis=0), keys, nt_dims,
                               preferred_element_type=F32)
        return jnp.where(jnp.concatenate([left, left], axis=0),
                         both[0:2 * CHUNK], both[2 * CHUNK:4 * CHUNK])

    decay = level(pair_decay)
    kq = level(pair_kq)
    a_mat = level(lambda p: jnp.where(strict, kq[p][0:CHUNK] * decay[p], 0.0))
    for p in pairs:
        qk = (kq[p][CHUNK:2 * CHUNK] * decay[p]).astype(BF16)
        for c, hd in heads_of(p):
            lhs2_ref[c, hd, 0:CHUNK, :] = qk

    def pdot(x, y_bd):
        return jnp.dot(x.astype(BF16), y_bd, preferred_element_type=F32)

    n1 = level(lambda p: jnp.where(same16, -a_mat[p], 0.0))
    n2 = level(lambda p: pdot(n1[p], blockdiag(n1[p])))
    t = level(lambda p: eye + n1[p])
    tn = level(lambda p: pdot(jnp.concatenate([t[p], n2[p]], axis=0), blockdiag(n2[p])))
    t = level(lambda p: t[p] + tn[p][0:CHUNK])
    n4 = level(lambda p: tn[p][CHUNK:2 * CHUNK])
    tn = level(lambda p: pdot(jnp.concatenate([t[p], n4[p]], axis=0), blockdiag(n4[p])))
    t = level(lambda p: t[p] + tn[p][0:CHUNK])
    t = level(lambda p: t[p] + pdot(t[p], blockdiag(tn[p][CHUNK:2 * CHUNK])))
    off = level(lambda p: jnp.where(same16, 0.0, a_mat[p]))
    m = level(lambda p: -pdot(off[p], blockdiag(t[p])))
    tm_ = level(lambda p: pdot(jnp.concatenate([t[p], m[p]], axis=0), blockdiag(m[p])))
    t = level(lambda p: t[p] + tm_[p][0:CHUNK])
    t = level(lambda p: (t[p] + pdot(t[p], blockdiag(tm_[p][CHUNK:2 * CHUNK]))).astype(BF16))
    for p in pairs:
        for c, hd in heads_of(p):
            uw = jnp.dot(t[p], rhs[(c, hd)], preferred_element_type=F32)
            u_ref[c, hd] = uw[:, 0:DN_HEAD_DIM]
            lhs1_ref[c, hd, 0:CHUNK, :] = uw[:, DN_HEAD_DIM:2 * DN_HEAD_DIM].astype(BF16)

    chains = [(b, hd) for b in range(bsz) for hd in range(DN_HEADS)]
    state = {ch: state_ref[ch[0] * DN_HEADS + ch[1]] for ch in chains}
    for c in range(n_chunks):
        r1, r2 = {}, {}
        for b, hd in chains:
            r1[b, hd] = jnp.dot(lhs1_ref[b * n_chunks + c, hd], state[b, hd].astype(BF16),
                                preferred_element_type=F32)
        for b, hd in chains:
            slot = b * n_chunks + c
            v_new = half_pad((u_ref[slot, hd] - r1[b, hd][0:CHUNK]).astype(BF16), hd, zeros_bf)
            r2[b, hd] = jnp.dot(lhs2_ref[slot, hd], v_new, preferred_element_type=F32)
        for b, hd in chains:
            slot = b * n_chunks + c
            gl = gl_ref[slot * DN_HEADS + hd:slot * DN_HEADS + hd + 1, :]
            state[b, hd] = state[b, hd] * gl + r2[b, hd][CHUNK:CHUNK + DN_HEAD_DIM]
            o = r1[b, hd][CHUNK:2 * CHUNK] + r2[b, hd][0:CHUNK]
            og_ref[head_tile(slot, hd)] = (
                _rms(o, onorm) * zs_ref[head_tile(slot, hd)].astype(F32)).astype(BF16)
    for b, hd in chains:
        state_ref[b * DN_HEADS + hd] = state[b, hd]


def _delta(qkv, zs, gates, gatest, layer, onorm, bsz):
    rows = qkv.shape[0]
    lp = rows // bsz
    n_chunks = TC // CHUNK
    n_slots = bsz * n_chunks

    def per_batch(a):
        return a.reshape((bsz, a.shape[0] // bsz) + a.shape[1:])

    def column_block(j):
        return pl.BlockSpec((bsz, TC, D_MODEL), lambda i: (0, i, j))

    row_spec = column_block(0)
    qkv = per_batch(qkv)
    og = pl.pallas_call(
        _delta_kernel,
        grid=(lp // TC,),
        in_specs=[
            column_block(0), column_block(1), column_block(2), row_spec,
            pl.BlockSpec((bsz, TC, LANES), lambda i: (0, i, 0)),
            pl.BlockSpec((bsz, n_chunks, 2 * DN_HEADS, CHUNK), lambda i: (0, i, 0, 0)),
            _layer_block(layer, (1, DN_HEAD_DIM)),
        ],
        out_specs=row_spec,
        out_shape=jax.ShapeDtypeStruct((bsz, lp, D_MODEL), BF16),
        scratch_shapes=[
            pltpu.VMEM((bsz * DN_HEADS, DN_HEAD_DIM, DN_HEAD_DIM), F32),
            pltpu.VMEM((n_slots, DN_HEADS, 2 * CHUNK, DN_HEAD_DIM), BF16),
            pltpu.VMEM((n_slots, DN_HEADS, CHUNK + DN_HEAD_DIM, 2 * CHUNK), BF16),
            pltpu.VMEM((n_slots, DN_HEADS, CHUNK, DN_HEAD_DIM), F32),
            pltpu.VMEM((n_slots * DN_HEADS, DN_HEAD_DIM), F32),
        ],
        compiler_params=pltpu.CompilerParams(
            dimension_semantics=("arbitrary",), vmem_limit_bytes=VMEM_LIMIT),
        name="delta_rule",
    )(qkv, qkv, qkv, *(per_batch(a) for a in (zs, gates, gatest)), onorm)
    return og.reshape(rows, D_MODEL)


FF_SPLITS = ((0, D_MODEL), (D_MODEL, 2 * D_MODEL), (2 * D_MODEL, D_FF))


def _ffn_kernel(*refs, tiles_per_batch, first, final):
    n_hidden = 2 if first else 1
    hidden = refs[:n_hidden]
    (og_ref, s_ref, ga_ref, gb_ref, wdn_ref, wsc_ref, wo_ref, n2_ref, wgu_ref, wdown_ref,
     fnorm_ref, out_ref) = refs[n_hidden:]
    tile_in_batch = pl.program_id(0) % tiles_per_batch

    def body():
        h = _input_tile(*hidden, tile_in_batch) if first else hidden[0][...]
        ya = jnp.dot(og_ref[...], wdn_ref[...].astype(BF16), preferred_element_type=F32)
        yb = jnp.dot(s_ref[...], wsc_ref[...].astype(BF16), preferred_element_type=F32)
        mixed = ga_ref[...].astype(F32) * ya + gb_ref[...].astype(F32) * yb
        h1 = h + _bdot(mixed, wo_ref[...])
        hn = _rms(h1, n2_ref[...]).astype(BF16)
        acc = h1
        for lo, hi in FF_SPLITS:
            gate = jnp.dot(hn, wgu_ref[:, lo:hi], preferred_element_type=F32)
            up = jnp.dot(hn, wgu_ref[:, D_FF + lo:D_FF + hi], preferred_element_type=F32)
            acc = acc + _bdot(_silu(gate) * up, wdown_ref[lo:hi, :])
        out_ref[...] = _rms(acc, fnorm_ref[...]) if final else acc

    if final:
        pl.when(tile_in_batch != 0)(body)
    else:
        body()


def _ffn(hidden, og, sgg, layer, wdn, wsc, wo, n2, wgu, wdown, fnorm, rows, tiles_per_batch,
         final):
    first = len(hidden) == 2
    row_spec = pl.BlockSpec((TM, D_MODEL), lambda i: (i, 0))

    def sgg_block(j):
        return pl.BlockSpec((TM, D_MODEL), lambda i: (i, j))

    if final:
        seq_tiles = tiles_per_batch - 1
        out_rows = rows // tiles_per_batch * seq_tiles
        out_spec = pl.BlockSpec(
            (TM, D_MODEL),
            lambda i: ((i // tiles_per_batch) * seq_tiles + jnp.maximum(i % tiles_per_batch - 1, 0), 0))
    else:
        out_rows, out_spec = rows, row_spec
    return pl.pallas_call(
        functools.partial(_ffn_kernel, tiles_per_batch=tiles_per_batch, first=first, final=final),
        grid=(rows // TM,),
        in_specs=_hidden_specs(first, tiles_per_batch) + [
            row_spec, sgg_block(0), sgg_block(1), sgg_block(2),
            _layer_block(layer, (D_MODEL, D_MODEL)),
            _layer_block(layer, (D_MODEL, D_MODEL)),
            _layer_block(layer, (D_MODEL, D_MODEL)),
            _layer_block(layer, (1, D_MODEL)),
            _layer_block(layer, (D_MODEL, 2 * D_FF)),
            _layer_block(layer, (D_FF, D_MODEL)),
            _resident((1, D_MODEL)),
        ],
        out_specs=out_spec,
        out_shape=jax.ShapeDtypeStruct((out_rows, D_MODEL), F32),
        compiler_params=pltpu.CompilerParams(
            dimension_semantics=("arbitrary",), vmem_limit_bytes=VMEM_LIMIT),
        name="mix_ffn",
    )(*hidden, og, sgg, sgg, sgg, wdn, wsc, wo, n2, wgu, wdown, fnorm)


def _pack_kernel(wt_ref, out_ref):
    g0, g1 = W_IN_GATES.start, W_IN_GATES.stop
    out_ref[:, 0:g0] = wt_ref[0:g0, :].T.astype(BF16)
    gates = wt_ref[g0:g0 + LANES, :].T
    lane = lax.broadcasted_iota(jnp.int32, gates.shape, 1)
    out_ref[:, g0:g0 + LANES] = jnp.where(lane < g1 - g0, gates, 0.0).astype(BF16)
    out_ref[:, g0 + LANES:] = wt_ref[g1:, :].T.astype(BF16)


def _pack_w_in(w_in):
    depth, d, width = w_in.shape
    return pl.pallas_call(
        _pack_kernel,
        grid=(depth, d // LANES),
        in_specs=[pl.BlockSpec((None, width, LANES), lambda l, i: (l, 0, i))],
        out_specs=pl.BlockSpec((None, LANES, PACKED_WIDTH), lambda l, i: (l, i, 0)),
        out_shape=jax.ShapeDtypeStruct((depth, d, PACKED_WIDTH), BF16),
        compiler_params=pltpu.CompilerParams(
            dimension_semantics=("arbitrary", "arbitrary"), vmem_limit_bytes=VMEM_LIMIT),
        name="pack_w_in",
    )(jnp.swapaxes(w_in, 1, 2))


def _gate_rows(p):
    return jnp.pad(p.astype(F32), ((0, 0), (DN_HEADS, LANES - 2 * DN_HEADS)))[:, None, :]


def kernel(x, meta_tokens, norm1, w_in, conv_qkv, a_log, dt_bias, o_norm, w_dn_out, conv_sc,
           w_sc_out, w_o, norm2, w_gate_up, w_down, final_norm):
    bsz, seq, d = x.shape
    depth = w_in.shape[0]
    assert d == D_MODEL and seq % TM == 0 and (seq + TM) % TC == 0 and TM >= N_META
    tiles_per_batch = seq // TM + 1
    rows = bsz * tiles_per_batch * TM

    w_packed = _pack_w_in(w_in)
    wdn, wsc, wo = w_dn_out, w_sc_out, w_o
    wgu, wdown = w_gate_up.astype(BF16), w_down
    n1, n2, onorm = norm1[:, None, :], norm2[:, None, :], o_norm[:, None, :]
    alog_rows, dtb_rows = _gate_rows(a_log), _gate_rows(dt_bias)
    fnorm = final_norm.reshape(1, d)

    hidden = (x.reshape(bsz * seq, d), meta_tokens.astype(x.dtype))
    for layer in range(depth):
        qkv, zs, gates, gatest, sgg = _inproj(
            hidden, layer, n1, w_packed, conv_qkv, conv_sc, alog_rows, dtb_rows,
            rows, tiles_per_batch)
        og = _delta(qkv, zs, gates, gatest, layer, onorm, bsz)
        h = _ffn(hidden, og, sgg, layer, wdn, wsc, wo, n2, wgu, wdown, fnorm,
                 rows, tiles_per_batch, layer == depth - 1)
        hidden = (h,)
    return hidden[0].reshape(bsz, seq, d)
```
